```python
import jax, jax.numpy as jnp
from jax import lax
import numpy as np

D_MODEL = 1024
BATCH = 16
SEQ = 2048
DEPTH = 4

CTX_LEN = 256
GRID_W = 64
HEAD_DIM = 64
A_HEADS = 8
A_KV_HEADS = 2
A_WINDOW = 128
A_BLOCK = 128
B_HEADS = 8
NA_KH = 8
NA_KW = 16
NA_QCOLS = 16
NA_BAND = 32
C_HEADS = 4
C_DK = 128
C_DV = 128
D_HEADS = 4
D_DK = 64
D_DV = 128
D_GATE_RANK = 16
D_GATE_NORM = 16.0
CHUNK = 32
N_BRANCH = 4
D_FF = 4 * D_MODEL
ROPE_BASE = 10000.0
EPS = 1e-6
NEG_INF = -1e30

A_QW = A_HEADS * HEAD_DIM
A_KVW = A_KV_HEADS * HEAD_DIM
B_W = B_HEADS * HEAD_DIM
C_KW = C_HEADS * C_DK
C_VW = C_HEADS * C_DV
D_KW = D_HEADS * D_DK
D_VW = D_HEADS * D_DV
BRANCH_W = 512
IN_NAMES = ('a_q', 'a_k', 'a_v', 'b_q', 'b_k', 'b_v', 'c_q', 'c_f_fwd', 'c_f_bwd', 'c_i', 'c_g',
            'd_q', 'd_k', 'd_v', 'd_gk_fwd', 'd_gk_bwd', 'd_g', 'merge')
IN_SIZES = (A_QW, A_KVW, A_KVW, B_W, B_W, B_W, C_KW, C_KW, C_KW, C_VW, C_VW,
            D_KW, D_KW, D_VW, D_GATE_RANK, D_GATE_RANK, D_VW, N_BRANCH * D_MODEL)
IN_WIDTH = sum(IN_SIZES)

kernel_name = 'hybrid_bidir_diffusion_trunk'


def rms_norm(x, g):
    xf = x.astype(jnp.float32)
    y = xf * lax.rsqrt(jnp.mean(jnp.square(xf), axis=-1, keepdims=True) + EPS)
    return (y * g.astype(jnp.float32)).astype(x.dtype)


def heads(t, nh):
    return t.reshape(t.shape[:-1] + (nh, t.shape[-1] // nh))


def flip(t):
    return t[:, ::-1]


def split_in(z):
    out = {}
    off = 0
    for name, size in zip(IN_NAMES, IN_SIZES):
        out[name] = z[..., off:off + size]
        off += size
    return out


def axial_angles(n):
    t = jnp.arange(n)
    row = (t // GRID_W).astype(jnp.float32)
    col = (t % GRID_W).astype(jnp.float32)
    half = HEAD_DIM // 2
    inv = ROPE_BASE ** (-jnp.arange(0, half, 2, dtype=jnp.float32) / half)
    return row[:, None] * inv, col[:, None] * inv


def rope_1d(x, ang):
    f = x.shape[-1] // 2
    cos = jnp.cos(ang)[:, None, :].astype(x.dtype)
    sin = jnp.sin(ang)[:, None, :].astype(x.dtype)
    x1, x2 = x[..., :f], x[..., f:]
    return jnp.concatenate([x1 * cos - x2 * sin, x2 * cos + x1 * sin], axis=-1)


def axial_rope(x, ang_r, ang_c):
    half = HEAD_DIM // 2
    return jnp.concatenate([rope_1d(x[..., :half], ang_r), rope_1d(x[..., half:], ang_c)], axis=-1)


def ctx_attention(q, k, v, sink):
    bsz, L, nh, hd = q.shape
    nkv = k.shape[2]
    grp = nh // nkv
    qg = q.reshape(bsz, L, nkv, grp, hd)
    s = jnp.einsum('bqkgd,bskd->bkgqs', qg, k).astype(jnp.float32) * (hd ** -0.5)
    if sink is not None:
        s_sink = jnp.broadcast_to(sink.reshape(nkv, grp)[None, :, :, None, None].astype(jnp.float32), (bsz, nkv, grp, L, 1))
        s = jnp.concatenate([s, s_sink], axis=-1)
    p = jax.nn.softmax(s, axis=-1)[..., :L].astype(v.dtype)
    o = jnp.einsum('bkgqs,bskd->bqkgd', p, v)
    return o.reshape(bsz, L, nh * hd)


def window_attention(q, k, v, kc, vc, sink):
    bsz, n, nh, hd = q.shape
    nkv = k.shape[2]
    grp = nh // nkv
    nb = n // A_BLOCK
    L = kc.shape[1]
    scale = hd ** -0.5
    qb = q.reshape(bsz, nb, A_BLOCK, nkv, grp, hd).swapaxes(0, 1)

    def bands(t):
        tp = jnp.pad(t, ((0, 0), (A_BLOCK, A_BLOCK), (0, 0), (0, 0))).reshape(bsz, nb + 2, A_BLOCK, nkv, hd)
        return jnp.concatenate([tp[:, :-2], tp[:, 1:-1], tp[:, 2:]], axis=2).swapaxes(0, 1)

    kb, vb = bands(k), bands(v)
    blk = jnp.arange(nb)[:, None, None]
    qpos = blk * A_BLOCK + jnp.arange(A_BLOCK)[None, :, None]
    kpos = (blk - 1) * A_BLOCK + jnp.arange(3 * A_BLOCK)[None, None, :]
    valid = (jnp.abs(kpos - qpos) <= A_WINDOW) & (kpos >= 0) & (kpos < n)
    sink_s = jnp.broadcast_to(sink.reshape(nkv, grp)[None, :, :, None, None].astype(jnp.float32),
                              (bsz, nkv, grp, A_BLOCK, 1))
    wlen = 3 * A_BLOCK

    def block(args):
        qi, ki, vi, mi = args
        s_win = jnp.einsum('bqkgd,bskd->bkgqs', qi, ki).astype(jnp.float32) * scale
        s_win = jnp.where(mi, s_win, NEG_INF)
        s_ctx = jnp.einsum('bqkgd,bckd->bkgqc', qi, kc).astype(jnp.float32) * scale
        p = jax.nn.softmax(jnp.concatenate([s_win, s_ctx, sink_s], axis=-1), axis=-1).astype(v.dtype)
        o = (jnp.einsum('bkgqs,bskd->bqkgd', p[..., :wlen], vi)
             + jnp.einsum('bkgqc,bckd->bqkgd', p[..., wlen:wlen + L], vc))
        return o.reshape(bsz, A_BLOCK, nh * hd)

    out = lax.map(block, (qb, kb, vb, valid))
    return out.swapaxes(0, 1).reshape(bsz, n, nh * hd)


def neighbourhood_attention(q, k, v, kc, vc, rel_bias):
    bsz, n, nh, hd = q.shape
    rows = n // GRID_W
    kh = min(NA_KH, rows)
    ncb = GRID_W // NA_QCOLS
    band_starts = [min(max(j * NA_QCOLS - NA_KW // 2, 0), GRID_W - NA_BAND) for j in range(ncb)]
    scale = hd ** -0.5
    qr_all = q.reshape(bsz, rows, ncb, NA_QCOLS, nh, hd).swapaxes(0, 1)
    kg = k.reshape(bsz, rows, GRID_W, nh, hd)
    vg = v.reshape(bsz, rows, GRID_W, nh, hd)
    qcol = (jnp.arange(ncb)[:, None] * NA_QCOLS + jnp.arange(NA_QCOLS)[None, :])[:, :, None]
    kcol = jnp.asarray(band_starts, dtype=jnp.int32)[:, None, None] + jnp.arange(NA_BAND)[None, None, :]
    cstart = jnp.clip(qcol - NA_KW // 2, 0, GRID_W - NA_KW)
    col_ok = (kcol >= cstart) & (kcol < cstart + NA_KW)
    dc_idx = jnp.clip(kcol - qcol, -(NA_KW - 1), NA_KW - 1) + (NA_KW - 1)
    wlen = kh * NA_BAND

    def row_block(args):
        r, qr = args
        r0 = jnp.clip(r - kh // 2, 0, rows - kh)
        kr = lax.dynamic_slice_in_dim(kg, r0, kh, axis=1)
        vr = lax.dynamic_slice_in_dim(vg, r0, kh, axis=1)
        kb = jnp.stack([kr[:, :, s:s + NA_BAND] for s in band_starts], axis=1)
        vb = jnp.stack([vr[:, :, s:s + NA_BAND] for s in band_starts], axis=1)
        dr_idx = r0 + jnp.arange(kh) - r + (NA_KH - 1)
        bias = rel_bias[:, dr_idx[None, None, :, None], dc_idx[:, :, None, :]]
        s_loc = jnp.einsum('bjqhd,bjrchd->bhjqrc', qr, kb).astype(jnp.float32) * scale
        s_loc = jnp.where(col_ok[:, :, None, :], s_loc + bias[None].astype(jnp.float32), NEG_INF)
        s_loc = s_loc.reshape(bsz, nh, ncb, NA_QCOLS, wlen)
        s_ctx = jnp.einsum('bjqhd,bchd->bhjqc', qr, kc).astype(jnp.float32) * scale
        p = jax.nn.softmax(jnp.concatenate([s_loc, s_ctx], axis=-1), axis=-1).astype(v.dtype)
        o = (jnp.einsum('bhjqs,bjshd->bjqhd', p[..., :wlen], vb.reshape(bsz, ncb, wlen, nh, hd))
             + jnp.einsum('bhjqc,bchd->bjqhd', p[..., wlen:], vc))
        return o.reshape(bsz, GRID_W, nh * hd)

    out = lax.map(row_block, (jnp.arange(rows), qr_all))
    return out.swapaxes(0, 1).reshape(bsz, n, nh * hd)


def chunk_scan(q, k, v, g, s0):
    bsz, n, nh, dk = k.shape
    dv = v.shape[-1]
    nc = n // CHUNK
    f32 = jnp.float32
    if s0 is None:
        s0 = jnp.zeros((bsz, nh, dk, dv), f32)

    def chunks(t):
        return t.astype(f32).reshape(bsz, nc, CHUNK, nh, t.shape[-1]).transpose(1, 0, 3, 2, 4)

    tri = jnp.tril(jnp.ones((CHUNK, CHUNK), dtype=bool))[:, :, None]

    def step(state, inp):
        qi, ki, vi, gi = inp
        cum = jnp.cumsum(gi, axis=2)
        rel = jnp.exp(jnp.where(tri, cum[:, :, :, None, :] - cum[:, :, None, :, :], NEG_INF))
        attn = jnp.einsum('bhtk,bhsk,bhtsk->bhts', qi, ki, rel)
        o = jnp.einsum('bhts,bhsv->bhtv', attn, vi) + jnp.einsum('bhtk,bhkv->bhtv', qi * jnp.exp(cum), state)
        last = cum[:, :, -1:, :]
        new_state = (jnp.exp(last[:, :, 0, :, None]) * state
                     + jnp.einsum('bhsk,bhsv->bhkv', ki * jnp.exp(last - cum), vi))
        return new_state, o

    s_fin, o = lax.scan(step, s0, (chunks(q), chunks(k), chunks(v), chunks(g)))
    return o.transpose(1, 0, 3, 2, 4).reshape(bsz, n, nh, dv).astype(v.dtype), s_fin


def final_state(k, v, g):
    cum = jnp.cumsum(g.astype(jnp.float32), axis=1)
    w = k.astype(jnp.float32) * jnp.exp(cum[:, -1:] - cum)
    return jnp.einsum('bnhk,bnhv->bhkv', w, v.astype(jnp.float32))


def bidir_recurrence(lat_f, lat_b, ctx_f, ctx_b, need_ctx):
    if need_ctx:
        oc_f, s_f = chunk_scan(*ctx_f, None)
        oc_b, s_b = chunk_scan(*[flip(t) for t in ctx_b], None)
        o_ctx = oc_f + flip(oc_b)
    else:
        s_f = final_state(*ctx_f[1:])
        s_b = final_state(*[flip(t) for t in ctx_b[1:]])
        o_ctx = None
    o_f, _ = chunk_scan(*lat_f, s_f)
    o_b, _ = chunk_scan(*[flip(t) for t in lat_b], s_b)
    return o_f + flip(o_b), o_ctx


def gated_group_norm(o, gain, gate_raw, nh):
    y = rms_norm(o, gain) * jax.nn.silu(heads(gate_raw, nh)).astype(o.dtype)
    return y.reshape(y.shape[:-2] + (-1,))


def merge_branches(branches, gate_logits, w_branch, w_out):
    g = heads(gate_logits, N_BRANCH)
    acc = None
    for j, yb in enumerate(branches):
        t = jax.nn.sigmoid(g[..., j, :]) * (yb @ w_branch[j])
        acc = t if acc is None else acc + t
    return acc @ w_out


def channel_mlp(h, w1, w2):
    return jnp.square(jax.nn.relu(h @ w1)) @ w2


def token_mixer(h, hc, w_in, a_sink, b_rel_bias, lb, c_norm, d_gate_up, d_gate_bias, d_norm,
                w_branch, w_out, need_ctx):
    n = h.shape[1]
    f32 = jnp.float32
    z = split_in(h @ w_in)
    zc = split_in(hc @ w_in)

    ang_r, ang_c = axial_angles(n)
    aq = axial_rope(heads(z['a_q'], A_HEADS), ang_r, ang_c)
    ak = axial_rope(heads(z['a_k'], A_KV_HEADS), ang_r, ang_c)
    akc, avc = heads(zc['a_k'], A_KV_HEADS), heads(zc['a_v'], A_KV_HEADS)
    y_a = window_attention(aq, ak, heads(z['a_v'], A_KV_HEADS), akc, avc, a_sink)

    bkc, bvc = heads(zc['b_k'], B_HEADS), heads(zc['b_v'], B_HEADS)
    y_b = neighbourhood_attention(heads(z['b_q'], B_HEADS), heads(z['b_k'], B_HEADS),
                                  heads(z['b_v'], B_HEADS), bkc, bvc, b_rel_bias)

    lbh = lb.reshape(C_HEADS, C_DK)

    def hgrn_forget(zf):
        zf = heads(zf, C_HEADS).astype(f32)
        log_f = jnp.log(lbh + (1.0 - lbh) * jax.nn.sigmoid(zf))
        key = (1.0 - lbh) * jax.nn.sigmoid(-zf)
        return key, log_f

    def hgrn_inputs(zz):
        q = jax.nn.silu(heads(zz['c_q'], C_HEADS))
        v = heads(zz['c_i'], C_HEADS)
        kf, gf = hgrn_forget(zz['c_f_fwd'])
        kb, gb = hgrn_forget(zz['c_f_bwd'])
        return (q, kf, v, gf), (q, kb, v, gb)

    lat_cf, lat_cb = hgrn_inputs(z)
    ctx_cf, ctx_cb = hgrn_inputs(zc)
    o_c, oc_c = bidir_recurrence(lat_cf, lat_cb, ctx_cf, ctx_cb, need_ctx)
    y_c = gated_group_norm(o_c, c_norm, z['c_g'], C_HEADS)

    def gla_inputs(zz):
        q = heads(zz['d_q'], D_HEADS) * (D_DK ** -0.5)
        k = heads(zz['d_k'], D_HEADS)
        v = heads(zz['d_v'], D_HEADS)

        def gate(zd, j):
            return heads(jax.nn.log_sigmoid((zd @ d_gate_up[j] + d_gate_bias[j]).astype(f32)) / D_GATE_NORM, D_HEADS)

        return (q, k, v, gate(zz['d_gk_fwd'], 0)), (q, k, v, gate(zz['d_gk_bwd'], 1))

    lat_df, lat_db = gla_inputs(z)
    ctx_df, ctx_db = gla_inputs(zc)
    o_d, oc_d = bidir_recurrence(lat_df, lat_db, ctx_df, ctx_db, need_ctx)
    y_d = gated_group_norm(o_d, d_norm, z['d_g'], D_HEADS)

    y = merge_branches([y_a, y_b, y_c, y_d], z['merge'], w_branch, w_out)
    if need_ctx:
        yc_a = ctx_attention(heads(zc['a_q'], A_HEADS), akc, avc, a_sink)
        yc_b = ctx_attention(heads(zc['b_q'], B_HEADS), bkc, bvc, None)
        yc_c = gated_group_norm(oc_c, c_norm, zc['c_g'], C_HEADS)
        yc_d = gated_group_norm(oc_d, d_norm, zc['d_g'], D_HEADS)
        y_ctx = merge_branches([yc_a, yc_b, yc_c, yc_d], zc['merge'], w_branch, w_out)
    else:
        y_ctx = None
    return y, y_ctx


def setup_inputs(seed: int = 0) -> dict:
    key = jax.random.key(seed)
    ks = jax.random.split(key, 20)
    f32 = jnp.float32

    def nrm(k, shape, s):
        return jax.random.normal(k, shape, f32) * s

    return {
        'x': nrm(ks[0], (BATCH, SEQ, D_MODEL), 1.0),
        'c': nrm(ks[1], (BATCH, D_MODEL), 1.0),
        'ctx': nrm(ks[2], (BATCH, CTX_LEN, D_MODEL), 1.0),
        'c_ctx': nrm(ks[3], (D_MODEL,), 1.0),
        'w_mod': nrm(ks[4], (DEPTH, D_MODEL, 6 * D_MODEL), 0.5 * D_MODEL ** -0.5),
        'b_mod': nrm(ks[5], (DEPTH, 6 * D_MODEL), 0.02),
        'norm_gains': 1.0 + nrm(ks[6], (DEPTH, 4, D_MODEL), 0.1),
        'w_in': nrm(ks[7], (DEPTH, D_MODEL, IN_WIDTH), D_MODEL ** -0.5),
        'a_sink': nrm(ks[8], (DEPTH, A_HEADS), 0.5),
        'b_rel_bias': nrm(ks[9], (DEPTH, B_HEADS, 2 * NA_KH - 1, 2 * NA_KW - 1), 0.5),
        'c_lower_bounds': nrm(ks[10], (DEPTH, C_KW), 0.1),
        'c_norm': 1.0 + nrm(ks[11], (DEPTH, C_DV), 0.1),
        'd_gate_up': nrm(ks[12], (DEPTH, 2, D_GATE_RANK, D_KW), D_GATE_RANK ** -0.5),
        'd_gate_bias': nrm(ks[13], (DEPTH, 2, D_KW), 0.1),
        'd_norm': 1.0 + nrm(ks[14], (DEPTH, D_DV), 0.1),
        'w_branch': nrm(ks[15], (DEPTH, N_BRANCH, BRANCH_W, D_MODEL), BRANCH_W ** -0.5),
        'w_out': nrm(ks[16], (DEPTH, D_MODEL, D_MODEL), D_MODEL ** -0.5),
        'w_ff1': nrm(ks[17], (DEPTH, D_MODEL, D_FF), D_MODEL ** -0.5),
        'w_ff2': nrm(ks[18], (DEPTH, D_FF, D_MODEL), D_FF ** -0.5),
    }


def reference(x, c, ctx, c_ctx, w_mod, b_mod, norm_gains, w_in, a_sink, b_rel_bias, c_lower_bounds,
              c_norm, d_gate_up, d_gate_bias, d_norm, w_branch, w_out, w_ff1, w_ff2):
    lb_soft = jax.nn.softmax(c_lower_bounds.astype(jnp.float32), axis=0)
    lb_all = jnp.cumsum(lb_soft, axis=0) - lb_soft[0:1]
    xc = ctx
    for l in range(DEPTH):
        need_ctx = l < DEPTH - 1
        ng = norm_gains[l]
        mod = jax.nn.silu(c) @ w_mod[l] + b_mod[l]
        mod_c = jax.nn.silu(c_ctx) @ w_mod[l] + b_mod[l]
        sh1, sc1, g1, sh2, sc2, g2 = jnp.split(mod[:, None, :], 6, axis=-1)
        csh1, csc1, cg1, csh2, csc2, cg2 = jnp.split(mod_c, 6, axis=-1)
        h = rms_norm(x, ng[0]) * (1.0 + sc1) + sh1
        hc = rms_norm(xc, ng[0]) * (1.0 + csc1) + csh1
        y, y_ctx = token_mixer(h, hc, w_in[l], a_sink[l], b_rel_bias[l], lb_all[l], c_norm[l],
                               d_gate_up[l], d_gate_bias[l], d_norm[l], w_branch[l], w_out[l], need_ctx)
        x = x + g1 * rms_norm(y, ng[1])
        h2 = rms_norm(x, ng[2]) * (1.0 + sc2) + sh2
        x = x + g2 * rms_norm(channel_mlp(h2, w_ff1[l], w_ff2[l]), ng[3])
        if need_ctx:
            xc = xc + cg1 * rms_norm(y_ctx, ng[1])
            hc2 = rms_norm(xc, ng[2]) * (1.0 + csc2) + csh2
            xc = xc + cg2 * rms_norm(channel_mlp(hc2, w_ff1[l], w_ff2[l]), ng[3])
    return x
```

```python
import functools

import numpy as np
import jax
import jax.numpy as jnp
from jax import lax
from jax.experimental import pallas as pl
from jax.experimental.pallas import tpu as pltpu

D_MODEL = 1024
SEQ = 2048
DEPTH = 4
CTX_LEN = 256
GRID_W = 64
HEAD_DIM = 64
A_HEADS = 8
A_KV_HEADS = 2
A_WINDOW = 128
A_BLOCK = 128
B_HEADS = 8
NA_KH = 8
NA_KW = 16
C_HEADS = 4
C_DK = 128
D_HEADS = 4
D_DK = 64
D_GATE_RANK = 16
D_GATE_NORM = 16.0
N_BRANCH = 4
D_FF = 4 * D_MODEL
ROPE_BASE = 10000.0
EPS = 1e-6
NEG_INF = -1e30
BRANCH_W = 512

F32 = jnp.float32
BF16 = jnp.bfloat16

LANES = 128
TM = 512
TM_POST = 256
FF_CHUNK = 1024
SCAN_T = 64
VMEM_LIMIT = 52 * 1024 * 1024

GD_WIDTH = 1664

_NT = (((1,), (1,)), ((), ()))
_TN = (((0,), (0,)), ((), ()))


def _cparams(sem):
    return pltpu.CompilerParams(dimension_semantics=sem, vmem_limit_bytes=VMEM_LIMIT)


def _rms(x, gain):
    return x * lax.rsqrt(jnp.mean(x * x, axis=-1, keepdims=True) + EPS) * gain


def _norm_mod(x, gain, sc, sh):
    return _rms(x, gain) * (1.0 + sc) + sh


def _silu(x):
    return x * jax.nn.sigmoid(x)


def _log_sigmoid(x):
    return jnp.minimum(x, 0.0) - jnp.log1p(jnp.exp(-jnp.abs(x)))


def _mod_kernel(c_ref, w_ref, b_ref, o_ref):
    a = _silu(c_ref[...]).astype(BF16)
    o_ref[0] = jnp.dot(a, w_ref[0].astype(BF16), preferred_element_type=F32) + b_ref[0]


def _modulation(c_rows, w_mod, b_mod):
    rows = c_rows.shape[0]
    return pl.pallas_call(
        _mod_kernel,
        grid=(DEPTH, 6),
        in_specs=[pl.BlockSpec((rows, D_MODEL), lambda l, j: (0, 0)),
                  pl.BlockSpec((1, D_MODEL, D_MODEL), lambda l, j: (l, 0, j)),
                  pl.BlockSpec((1, 1, D_MODEL), lambda l, j: (l, 0, j))],
        out_specs=pl.BlockSpec((1, rows, D_MODEL), lambda l, j: (l, 0, j)),
        out_shape=jax.ShapeDtypeStruct((DEPTH, rows, 6 * D_MODEL), F32),
        compiler_params=_cparams(("parallel", "parallel")),
        name="modulation",
    )(c_rows, w_mod, b_mod.reshape(DEPTH, 1, 6 * D_MODEL))


def _in_proj_kernel(x_ref, mod_ref, gain_ref, w_ref, *rest, rope_slabs):
    if rope_slabs:
        cos_ref, s1_ref, s2_ref, o_ref, h_scr = rest
    else:
        o_ref, h_scr = rest

    @pl.when(pl.program_id(1) == 0)
    def _():
        m = mod_ref[0]
        h = _norm_mod(x_ref[...], gain_ref[0:1, :], m[1:2, :], m[0:1, :])
        h_scr[...] = h.astype(BF16)

    z = jnp.dot(h_scr[...], w_ref[...], preferred_element_type=F32)
    if rope_slabs:
        cos, s1, s2 = cos_ref[...], s1_ref[...], s2_ref[...]
        for s in range(rope_slabs):
            zs = z[:, s * LANES:(s + 1) * LANES]
            r = (zs * cos + pltpu.roll(zs, LANES - 16, axis=1) * s1
                 + pltpu.roll(zs, 16, axis=1) * s2)
            o_ref[:, s * LANES:(s + 1) * LANES] = r.astype(o_ref.dtype)
        rest_lo = rope_slabs * LANES
        o_ref[:, rest_lo:] = z[:, rest_lo:].astype(o_ref.dtype)
    else:
        o_ref[...] = z.astype(o_ref.dtype)


def _in_proj(xs, mod, gains, w, tn, out_dtype, bsz, rope=None):
    rows = xs.shape[0]
    width = w.shape[1]
    n_lat_tiles = bsz * SEQ // TM
    tiles_per_seq = SEQ // TM

    def mod_idx(i, j):
        return (jnp.where(i < n_lat_tiles, i // tiles_per_seq, bsz), 0, 0)

    in_specs = [pl.BlockSpec((TM, D_MODEL), lambda i, j: (i, 0)),
                pl.BlockSpec((1, 8, D_MODEL), mod_idx),
                pl.BlockSpec((4, D_MODEL), lambda i, j: (0, 0)),
                pl.BlockSpec((D_MODEL, tn), lambda i, j: (0, j))]
    args = [xs, mod, gains, w]
    slabs = 0
    if rope is not None:
        slabs = 5

        def rope_idx(i, j):
            return (jnp.where(i < n_lat_tiles, i % tiles_per_seq, tiles_per_seq), 0)

        in_specs += [pl.BlockSpec((TM, LANES), rope_idx)] * 3
        args += list(rope)
    return pl.pallas_call(
        functools.partial(_in_proj_kernel, rope_slabs=slabs),
        grid=(rows // TM, width // tn),
        in_specs=in_specs,
        out_specs=pl.BlockSpec((TM, tn), lambda i, j: (i, j)),
        out_shape=jax.ShapeDtypeStruct((rows, width), out_dtype),
        scratch_shapes=[pltpu.VMEM((TM, D_MODEL), BF16)],
        compiler_params=_cparams(("parallel", "arbitrary")),
        name="in_proj_rope" if slabs else "in_proj",
    )(*args)


def _rope_tables():
    t = jnp.arange(SEQ)
    row = (t // GRID_W).astype(F32)
    col = (t % GRID_W).astype(F32)
    half = HEAD_DIM // 2
    inv = ROPE_BASE ** (-jnp.arange(0, half, 2, dtype=F32) / half)
    lane = np.arange(LANES)
    hl = lane % HEAD_DIM
    use_row = (hl // half) == 0
    freq = hl % (half // 2)
    first = (hl % half) < (half // 2)
    ang = jnp.where(use_row[None, :], row[:, None], col[:, None]) * inv[freq][None, :]
    cos, sin = jnp.cos(ang), jnp.sin(ang)
    s1 = jnp.where(first[None, :], -sin, 0.0)
    s2 = jnp.where(first[None, :], 0.0, sin)
    ident = jnp.ones((TM, LANES), F32)
    zero = jnp.zeros((TM, LANES), F32)
    return (jnp.concatenate([cos, ident]), jnp.concatenate([s1, zero]), jnp.concatenate([s2, zero]))


def _attn_a_kernel(sink_ref, q_ref, kp_ref, kc_ref, kn_ref, vp_ref, vc_ref, vn_ref, kx_ref, vx_ref, o_ref):
    i = pl.program_id(1)
    grp = A_HEADS // A_KV_HEADS
    rows = grp * A_BLOCK
    nwin = 3 * A_BLOCK
    nkeys = nwin + CTX_LEN
    scale = HEAD_DIM ** -0.5
    r_io = lax.broadcasted_iota(jnp.int32, (rows, nkeys), 0)
    k_io = lax.broadcasted_iota(jnp.int32, (rows, nkeys), 1)
    qpos = i * A_BLOCK + (r_io & (A_BLOCK - 1))
    kpos = (i - 1) * A_BLOCK + k_io
    valid = ((jnp.abs(kpos - qpos) <= A_WINDOW) & (kpos >= 0) & (kpos < SEQ)) | (k_io >= nwin)
    head_of_row = lax.broadcasted_iota(jnp.int32, (rows, 1), 0) >> 7
    for g in range(A_KV_HEADS):
        hs = slice(g * HEAD_DIM, (g + 1) * HEAD_DIM)
        q = jnp.concatenate([q_ref[:, (g * grp + j) * HEAD_DIM:(g * grp + j + 1) * HEAD_DIM]
                             for j in range(grp)], axis=0)
        k = jnp.concatenate([kp_ref[:, hs], kc_ref[:, hs], kn_ref[:, hs], kx_ref[:, hs]], axis=0)
        v = jnp.concatenate([vp_ref[:, hs], vc_ref[:, hs], vn_ref[:, hs], vx_ref[:, hs]], axis=0)
        s = lax.dot_general(q, k, _NT, preferred_element_type=F32) * scale
        s = jnp.where(valid, s, NEG_INF)
        sink = jnp.zeros((rows, 1), F32)
        for j in range(grp):
            sink = jnp.where(head_of_row == j, sink_ref[g * grp + j], sink)
        m = jnp.maximum(jnp.max(s, axis=-1, keepdims=True), sink)
        p = jnp.exp(s - m)
        den = jnp.sum(p, axis=-1, keepdims=True) + jnp.exp(sink - m)
        o = jnp.dot(p.astype(BF16), v, preferred_element_type=F32) / den
        o = jnp.concatenate([o[j * A_BLOCK:(j + 1) * A_BLOCK] for j in range(grp)], axis=1)
        o_ref[:, g * grp * HEAD_DIM:(g + 1) * grp * HEAD_DIM] = o.astype(o_ref.dtype)


def _attn_a(za, sink, bsz):
    nb = SEQ // A_BLOCK
    qcb = A_HEADS * HEAD_DIM // LANES
    ctx_blk0 = bsz * SEQ // CTX_LEN

    def kv_spec(col, off):
        return pl.BlockSpec((A_BLOCK, LANES),
                            lambda b, i: (b * nb + jnp.clip(i + off, 0, nb - 1), col))

    return pl.pallas_call(
        _attn_a_kernel,
        grid=(bsz, nb),
        in_specs=[pl.BlockSpec(memory_space=pltpu.SMEM),
                  pl.BlockSpec((A_BLOCK, A_HEADS * HEAD_DIM), lambda b, i: (b * nb + i, 0)),
                  kv_spec(qcb, -1), kv_spec(qcb, 0), kv_spec(qcb, 1),
                  kv_spec(qcb + 1, -1), kv_spec(qcb + 1, 0), kv_spec(qcb + 1, 1),
                  pl.BlockSpec((CTX_LEN, LANES), lambda b, i: (ctx_blk0 + b, qcb)),
                  pl.BlockSpec((CTX_LEN, LANES), lambda b, i: (ctx_blk0 + b, qcb + 1))],
        out_specs=pl.BlockSpec((A_BLOCK, BRANCH_W), lambda b, i: (b * nb + i, 0)),
        out_shape=jax.ShapeDtypeStruct((bsz * SEQ, BRANCH_W), BF16),
        compiler_params=_cparams(("parallel", "parallel")),
        name="attn_window",
    )(sink, za, za, za, za, za, za, za, za, za)


def _attn_b_kernel(q_ref, k_ref, v_ref, kx_ref, vx_ref, bias_ref, o_ref):
    r = pl.program_id(1)
    rows = SEQ // GRID_W
    nwin = NA_KH * GRID_W
    r0 = jnp.clip(r - NA_KH // 2, 0, rows - NA_KH)
    start = pl.multiple_of(r0 * GRID_W, GRID_W)
    kw = k_ref[pl.ds(start, nwin), :]
    vw = v_ref[pl.ds(start, nwin), :]
    scale = HEAD_DIM ** -0.5
    outs = []
    for h in range(B_HEADS):
        hs = slice(h * HEAD_DIM, (h + 1) * HEAD_DIM)
        q = q_ref[:, hs]
        s_w = lax.dot_general(q, kw[:, hs], _NT, preferred_element_type=F32) * scale + bias_ref[0, h]
        s_c = lax.dot_general(q, kx_ref[:, hs], _NT, preferred_element_type=F32) * scale
        m = jnp.maximum(jnp.max(s_w, axis=-1, keepdims=True), jnp.max(s_c, axis=-1, keepdims=True))
        p_w = jnp.exp(s_w - m)
        p_c = jnp.exp(s_c - m)
        den = jnp.sum(p_w, axis=-1, keepdims=True) + jnp.sum(p_c, axis=-1, keepdims=True)
        o = (jnp.dot(p_w.astype(BF16), vw[:, hs], preferred_element_type=F32)
             + jnp.dot(p_c.astype(BF16), vx_ref[:, hs], preferred_element_type=F32)) / den
        outs.append(o)
    o_ref[...] = jnp.concatenate(outs, axis=1).astype(o_ref.dtype)


def _na_bias_classes(rel_bias):
    rows = SEQ // GRID_W
    qc = np.arange(GRID_W)[:, None]
    kc = np.arange(GRID_W)[None, :]
    cstart = np.clip(qc - NA_KW // 2, 0, GRID_W - NA_KW)
    col_ok = (kc >= cstart) & (kc < cstart + NA_KW)
    dc_idx = np.clip(kc - qc, -(NA_KW - 1), NA_KW - 1) + (NA_KW - 1)
    class_rows = [0, 1, 2, 3, NA_KH // 2, rows - 4, rows - 3, rows - 2, rows - 1]
    out = []
    for r in class_rows:
        r0 = min(max(r - NA_KH // 2, 0), rows - NA_KH)
        dr_idx = r0 + np.arange(NA_KH) - r + (NA_KH - 1)
        b = rel_bias[:, dr_idx[:, None, None], dc_idx[None, :, :]]
        b = jnp.where(col_ok[None, None], b, NEG_INF)
        out.append(b.transpose(0, 2, 1, 3).reshape(B_HEADS, GRID_W, NA_KH * GRID_W))
    return jnp.stack(out).astype(F32)


def _attn_b(zb, bias_cls, bsz):
    rows = SEQ // GRID_W
    ctx_blk0 = bsz * SEQ // CTX_LEN

    def cls_idx(b, r):
        return (jnp.where(r < 4, r, jnp.where(r >= rows - 4, r - (rows - 9), 4)), 0, 0, 0)

    return pl.pallas_call(
        _attn_b_kernel,
        grid=(bsz, rows),
        in_specs=[pl.BlockSpec((GRID_W, BRANCH_W), lambda b, r: (b * rows + r, 0)),
                  pl.BlockSpec((SEQ, BRANCH_W), lambda b, r: (b, 1)),
                  pl.BlockSpec((SEQ, BRANCH_W), lambda b, r: (b, 2)),
                  pl.BlockSpec((CTX_LEN, BRANCH_W), lambda b, r: (ctx_blk0 + b, 1)),
                  pl.BlockSpec((CTX_LEN, BRANCH_W), lambda b, r: (ctx_blk0 + b, 2)),
                  pl.BlockSpec((1, B_HEADS, GRID_W, NA_KH * GRID_W), cls_idx)],
        out_specs=pl.BlockSpec((GRID_W, BRANCH_W), lambda b, r: (b * rows + r, 0)),
        out_shape=jax.ShapeDtypeStruct((bsz * SEQ, BRANCH_W), BF16),
        compiler_params=_cparams(("parallel", "arbitrary")),
        name="attn_neighbourhood",
    )(zb, zb, zb, zb, zb, bias_cls)


def _attn_ctx_kernel(sink_ref, q_ref, k_ref, v_ref, o_ref, *, n_heads, n_kv):
    grp = n_heads // n_kv
    rows = grp * CTX_LEN
    scale = HEAD_DIM ** -0.5
    head_of_row = lax.broadcasted_iota(jnp.int32, (rows, 1), 0) >> 8
    for g in range(n_kv):
        hs = slice(g * HEAD_DIM, (g + 1) * HEAD_DIM)
        q = jnp.concatenate([q_ref[:, (g * grp + j) * HEAD_DIM:(g * grp + j + 1) * HEAD_DIM]
                             for j in range(grp)], axis=0)
        s = lax.dot_general(q, k_ref[:, hs], _NT, preferred_element_type=F32) * scale
        sink = jnp.zeros((rows, 1), F32)
        for j in range(grp):
            sink = jnp.where(head_of_row == j, sink_ref[g * grp + j], sink)
        m = jnp.maximum(jnp.max(s, axis=-1, keepdims=True), sink)
        p = jnp.exp(s - m)
        den = jnp.sum(p, axis=-1, keepdims=True) + jnp.exp(sink - m)
        o = jnp.dot(p.astype(BF16), v_ref[:, hs], preferred_element_type=F32) / den
        o = jnp.concatenate([o[j * CTX_LEN:(j + 1) * CTX_LEN] for j in range(grp)], axis=1)
        o_ref[:, g * grp * HEAD_DIM:(g + 1) * grp * HEAD_DIM] = o.astype(o_ref.dtype)


def _attn_ctx(z, sink, bsz, n_heads, n_kv):
    ctx_blk0 = bsz * SEQ // CTX_LEN
    kvw = n_kv * HEAD_DIM
    kcol = n_heads * HEAD_DIM // kvw
    return pl.pallas_call(
        functools.partial(_attn_ctx_kernel, n_heads=n_heads, n_kv=n_kv),
        grid=(bsz,),
        in_specs=[pl.BlockSpec(memory_space=pltpu.SMEM),
                  pl.BlockSpec((CTX_LEN, n_heads * HEAD_DIM), lambda b: (ctx_blk0 + b, 0)),
                  pl.BlockSpec((CTX_LEN, kvw), lambda b: (ctx_blk0 + b, kcol)),
                  pl.BlockSpec((CTX_LEN, kvw), lambda b: (ctx_blk0 + b, kcol + 1))],
        out_specs=pl.BlockSpec((CTX_LEN, BRANCH_W), lambda b: (b, 0)),
        out_shape=jax.ShapeDtypeStruct((bsz * CTX_LEN, BRANCH_W), BF16),
        compiler_params=_cparams(("parallel",)),
        name="attn_ctx_%d" % n_kv,
    )(sink, z, z, z)


def _scan_chunk(q, k, v, g, s_ref, d, tri, n_heads, dk):
    t = q.shape[0]
    trif = tri.astype(F32)
    cum = jnp.dot(trif, g, preferred_element_type=F32, precision=lax.Precision.HIGHEST)
    tot = cum[t - 1:t] if d == 0 else cum[0:1]
    mid = 0.5 * tot
    qa = (q * jnp.exp(cum - mid)).astype(BF16)
    kb = (k * jnp.exp(mid - cum)).astype(BF16)
    qd = (q * jnp.exp(cum)).astype(BF16)
    kd = (k * jnp.exp(tot - cum)).astype(BF16)
    vb = v.astype(BF16)
    tot_col = lax.dot_general(g, jnp.ones((t, LANES), F32), _TN, preferred_element_type=F32,
                              precision=lax.Precision.HIGHEST)
    decay = jnp.exp(tot_col)
    outs = []
    for h in range(n_heads):
        ks = slice(h * dk, (h + 1) * dk)
        vs = slice(h * LANES, (h + 1) * LANES)
        a = lax.dot_general(qa[:, ks], kb[:, ks], _NT, preferred_element_type=F32)
        a = jnp.where(tri, a, 0.0).astype(BF16)
        state = s_ref[d, h]
        o = (jnp.dot(a, vb[:, vs], preferred_element_type=F32)
             + jnp.dot(qd[:, ks], state.astype(BF16), preferred_element_type=F32))
        outs.append(o)
        s_ref[d, h] = decay[ks, :] * state + lax.dot_general(kd[:, ks], vb[:, vs], _TN,
                                                             preferred_element_type=F32)
    return jnp.concatenate(outs, axis=1)


def _scan_kernel(cq_f, cf_f, ci_f, cq_b, cf_b, ci_b, dqk_f, dv_f, dgk_f, dqk_b, dv_b, dgk_b,
                 lb_ref, up_ref, gb_ref, ocf_ref, ocb_ref, odf_ref, odb_ref, sc_scr, sd_scr):
    @pl.when(pl.program_id(1) == 0)
    def _():
        sc_scr[...] = jnp.zeros_like(sc_scr)
        sd_scr[...] = jnp.zeros_like(sd_scr)

    t = SCAN_T
    row = lax.broadcasted_iota(jnp.int32, (t, t), 0)
    col = lax.broadcasted_iota(jnp.int32, (t, t), 1)
    lb = lb_ref[...]
    dirs = ((cq_f, cf_f, ci_f, dqk_f, dv_f, dgk_f, ocf_ref, odf_ref),
            (cq_b, cf_b, ci_b, dqk_b, dv_b, dgk_b, ocb_ref, odb_ref))
    for d, (cq, cf, ci, dqk, dv, dgk, oc_ref, od_ref) in enumerate(dirs):
        tri = (col <= row) if d == 0 else (col >= row)
        zf = cf[...]
        g = jnp.log(lb + (1.0 - lb) * jax.nn.sigmoid(zf))
        key = (1.0 - lb) * jax.nn.sigmoid(-zf)
        oc_ref[...] = _scan_chunk(_silu(cq[...]), key, ci[...], g, sc_scr, d, tri, C_HEADS, C_DK)
        zg = dgk[...][:, d * D_GATE_RANK:(d + 1) * D_GATE_RANK]
        graw = jnp.dot(zg, up_ref[d], preferred_element_type=F32,
                       precision=lax.Precision.HIGHEST) + gb_ref[d]
        g = _log_sigmoid(graw) / D_GATE_NORM
        qk = dqk[...]
        kw = D_HEADS * D_DK
        od_ref[...] = _scan_chunk(qk[:, :kw] * (D_DK ** -0.5), qk[:, kw:], dv[...], g, sd_scr, d, tri,
                                  D_HEADS, D_DK)


def _scan(zc, zd, lb, gate_up, gate_bias, bsz):
    t = SCAN_T
    n_ctx = CTX_LEN // t
    n_lat = SEQ // t
    ctx_blk0 = bsz * n_lat

    def fwd(b, c):
        return jnp.where(c < n_ctx, ctx_blk0 + b * n_ctx + c, b * n_lat + c - n_ctx)

    def bwd(b, c):
        return jnp.where(c < n_ctx, ctx_blk0 + b * n_ctx + (n_ctx - 1 - c), b * n_lat + (n_lat + n_ctx - 1 - c))

    def spec(order, width, col):
        return pl.BlockSpec((t, width), lambda b, c: (order(b, c), col))

    gk_col = (GD_WIDTH - LANES) // LANES
    in_specs = [spec(fwd, 512, 0), spec(fwd, 512, 1), spec(fwd, 512, 3),
                spec(bwd, 512, 0), spec(bwd, 512, 2), spec(bwd, 512, 3),
                spec(fwd, 512, 0), spec(fwd, 512, 1), spec(fwd, LANES, gk_col),
                spec(bwd, 512, 0), spec(bwd, 512, 1), spec(bwd, LANES, gk_col),
                pl.BlockSpec((1, 512), lambda b, c: (0, 0)),
                pl.BlockSpec((2, D_GATE_RANK, D_HEADS * D_DK), lambda b, c: (0, 0, 0)),
                pl.BlockSpec((2, 1, D_HEADS * D_DK), lambda b, c: (0, 0, 0))]
    rows = zc.shape[0]
    out = jax.ShapeDtypeStruct((rows, BRANCH_W), F32)
    return pl.pallas_call(
        _scan_kernel,
        grid=(bsz, n_ctx + n_lat),
        in_specs=in_specs,
        out_specs=[spec(fwd, 512, 0), spec(bwd, 512, 0), spec(fwd, 512, 0), spec(bwd, 512, 0)],
        out_shape=[out, out, out, out],
        scratch_shapes=[pltpu.VMEM((2, C_HEADS, C_DK, LANES), F32),
                        pltpu.VMEM((2, D_HEADS, D_DK, LANES), F32)],
        compiler_params=_cparams(("parallel", "arbitrary")),
        name="bidir_scan",
    )(zc, zc, zc, zc, zc, zc, zd, zd, zd, zd, zd, zd, lb, gate_up, gate_bias)


def _post_kernel(x_ref, mod_ref, gain_ref, ya_ref, yb_ref, ocf_ref, ocb_ref, odf_ref, odb_ref,
                 cg_ref, dg_ref, cn_ref, dn_ref, wm_ref, wb_ref, wo_ref, o_ref):
    x = x_ref[...]
    m = mod_ref[0]
    h = _norm_mod(x, gain_ref[0:1, :], m[1:2, :], m[0:1, :]).astype(BF16)

    def gated_group_norm(o, gain, gate):
        parts = [_rms(o[:, j * LANES:(j + 1) * LANES], gain) for j in range(BRANCH_W // LANES)]
        return (jnp.concatenate(parts, axis=1) * _silu(gate)).astype(BF16)

    y_c = gated_group_norm(ocf_ref[...] + ocb_ref[...], cn_ref[...], cg_ref[...])
    y_d = gated_group_norm(odf_ref[...] + odb_ref[...], dn_ref[...], dg_ref[...])
    branches = (ya_ref[...], yb_ref[...], y_c, y_d)
    acc = None
    for j, yb in enumerate(branches):
        gate = jnp.dot(h, wm_ref[:, j * D_MODEL:(j + 1) * D_MODEL], preferred_element_type=F32)
        term = jax.nn.sigmoid(gate) * jnp.dot(yb, wb_ref[j], preferred_element_type=F32)
        acc = term if acc is None else acc + term
    y = jnp.dot(acc.astype(BF16), wo_ref[...], preferred_element_type=F32)
    o_ref[...] = x + m[2:3, :] * _rms(y, gain_ref[1:2, :])


def _post(xs, mod, gains, ya, yb, ocf, ocb, odf, odb, zc, zd, c_norm, d_norm, wm, wb, wo, bsz, n_rows):
    tm = TM_POST
    n_lat_tiles = bsz * SEQ // tm
    tiles_per_seq = SEQ // tm

    def mod_idx(i):
        return (jnp.where(i < n_lat_tiles, i // tiles_per_seq, bsz), 0, 0)

    row = lambda w, col=0: pl.BlockSpec((tm, w), lambda i: (i, col))
    const = lambda shape: pl.BlockSpec(shape, lambda i: (0,) * len(shape))
    return pl.pallas_call(
        _post_kernel,
        grid=(n_rows // tm,),
        in_specs=[row(D_MODEL), pl.BlockSpec((1, 8, D_MODEL), mod_idx), const((4, D_MODEL)),
                  row(BRANCH_W), row(BRANCH_W), row(BRANCH_W), row(BRANCH_W), row(BRANCH_W), row(BRANCH_W),
                  row(BRANCH_W, 4), row(BRANCH_W, 2),
                  const((1, LANES)), const((1, LANES)),
                  const((D_MODEL, N_BRANCH * D_MODEL)), const((N_BRANCH, BRANCH_W, D_MODEL)),
                  const((D_MODEL, D_MODEL))],
        out_specs=row(D_MODEL),
        out_shape=jax.ShapeDtypeStruct((n_rows, D_MODEL), F32),
        compiler_params=_cparams(("parallel",)),
        name="merge_out",
    )(xs, mod, gains, ya, yb, ocf, ocb, odf, odb, zc, zd, c_norm, d_norm, wm, wb, wo)


def _ffn_kernel(x_ref, mod_ref, gain_ref, w1_ref, w2_ref, o_ref, h_scr, acc_scr):
    k = pl.program_id(1)

    @pl.when(k == 0)
    def _():
        m = mod_ref[0]
        h_scr[...] = _norm_mod(x_ref[...], gain_ref[2:3, :], m[4:5, :], m[3:4, :]).astype(BF16)
        acc_scr[...] = jnp.zeros_like(acc_scr)

    u = jnp.dot(h_scr[...], w1_ref[...], preferred_element_type=F32)
    u = jnp.square(jnp.maximum(u, 0.0)).astype(BF16)
    acc_scr[...] += jnp.dot(u, w2_ref[...], preferred_element_type=F32)

    @pl.when(k == pl.num_programs(1) - 1)
    def _():
        m = mod_ref[0]
        o_ref[...] = x_ref[...] + m[5:6, :] * _rms(acc_scr[...], gain_ref[3:4, :])


def _ffn(xs, mod, gains, w1, w2, bsz, n_rows):
    n_lat_tiles = bsz * SEQ // TM
    tiles_per_seq = SEQ // TM

    def mod_idx(i, k):
        return (jnp.where(i < n_lat_tiles, i // tiles_per_seq, bsz), 0, 0)

    return pl.pallas_call(
        _ffn_kernel,
        grid=(n_rows // TM, D_FF // FF_CHUNK),
        in_specs=[pl.BlockSpec((TM, D_MODEL), lambda i, k: (i, 0)),
                  pl.BlockSpec((1, 8, D_MODEL), mod_idx),
                  pl.BlockSpec((4, D_MODEL), lambda i, k: (0, 0)),
                  pl.BlockSpec((D_MODEL, FF_CHUNK), lambda i, k: (0, k)),
                  pl.BlockSpec((FF_CHUNK, D_MODEL), lambda i, k: (k, 0))],
        out_specs=pl.BlockSpec((TM, D_MODEL), lambda i, k: (i, 0)),
        out_shape=jax.ShapeDtypeStruct((n_rows, D_MODEL), F32),
        scratch_shapes=[pltpu.VMEM((TM, D_MODEL), BF16), pltpu.VMEM((TM, D_MODEL), F32)],
        compiler_params=_cparams(("parallel", "arbitrary")),
        name="mlp",
    )(xs, mod, gains, w1, w2)


def _split_w_in(w_in_l):
    a_w = A_HEADS * HEAD_DIM + 2 * A_KV_HEADS * HEAD_DIM
    b_w = 3 * B_HEADS * HEAD_DIM
    c_w = 5 * C_HEADS * C_DK
    o_a, o_b, o_c = 0, a_w, a_w + b_w
    o_d = o_c + c_w
    kw = D_HEADS * D_DK
    d_q = w_in_l[:, o_d:o_d + kw]
    d_k = w_in_l[:, o_d + kw:o_d + 2 * kw]
    d_v = w_in_l[:, o_d + 2 * kw:o_d + 2 * kw + 512]
    o_gk = o_d + 2 * kw + 512
    d_gk = w_in_l[:, o_gk:o_gk + 2 * D_GATE_RANK]
    d_g = w_in_l[:, o_gk + 2 * D_GATE_RANK:o_gk + 2 * D_GATE_RANK + 512]
    o_m = o_gk + 2 * D_GATE_RANK + 512
    pad = jnp.zeros((D_MODEL, LANES - 2 * D_GATE_RANK), w_in_l.dtype)
    g_d = jnp.concatenate([d_q, d_k, d_v, d_g, d_gk, pad], axis=1)
    groups = (w_in_l[:, o_a:o_b], w_in_l[:, o_b:o_c], w_in_l[:, o_c:o_d], g_d, w_in_l[:, o_m:])
    return tuple(g.astype(BF16) for g in groups)


def kernel(x, c, ctx, c_ctx, w_mod, b_mod, norm_gains, w_in, a_sink, b_rel_bias, c_lower_bounds, c_norm,
           d_gate_up, d_gate_bias, d_norm, w_branch, w_out, w_ff1, w_ff2):
    bsz = x.shape[0]
    n_lat = bsz * SEQ
    n_rows = n_lat + bsz * CTX_LEN

    lb_soft = jax.nn.softmax(c_lower_bounds.astype(F32), axis=0)
    lb_all = jnp.cumsum(lb_soft, axis=0) - lb_soft[0:1]

    c_rows = jnp.concatenate([c, c_ctx[None, :], jnp.zeros((-(bsz + 1) % 8, D_MODEL), F32)], axis=0)
    mod_all = _modulation(c_rows, w_mod, b_mod)[:, :bsz + 1]
    mod_all = mod_all.reshape(DEPTH, bsz + 1, 6, D_MODEL)
    mod_all = jnp.pad(mod_all, ((0, 0), (0, 0), (0, 2), (0, 0)))

    rope = _rope_tables()
    no_sink = jnp.full((B_HEADS,), NEG_INF, F32)
    xs = jnp.concatenate([x.reshape(n_lat, D_MODEL), ctx.reshape(bsz * CTX_LEN, D_MODEL)], axis=0)

    for l in range(DEPTH):
        need_ctx = l < DEPTH - 1
        mod, gains = mod_all[l], norm_gains[l]
        w_a, w_b, w_c, w_d, w_m = _split_w_in(w_in[l])
        za = _in_proj(xs, mod, gains, w_a, w_a.shape[1], BF16, bsz, rope=rope)
        zb = _in_proj(xs, mod, gains, w_b, 512, BF16, bsz)
        zc = _in_proj(xs, mod, gains, w_c, 512, F32, bsz)
        zd = _in_proj(xs, mod, gains, w_d, GD_WIDTH, F32, bsz)

        ya = _attn_a(za, a_sink[l], bsz)
        yb = _attn_b(zb, _na_bias_classes(b_rel_bias[l]), bsz)
        if need_ctx:
            ya = jnp.concatenate([ya, _attn_ctx(za, a_sink[l], bsz, A_HEADS, A_KV_HEADS)], axis=0)
            yb = jnp.concatenate([yb, _attn_ctx(zb, no_sink, bsz, B_HEADS, B_HEADS)], axis=0)
        ocf, ocb, odf, odb = _scan(zc, zd, lb_all[l].reshape(1, -1), d_gate_up[l],
                                   d_gate_bias[l].reshape(2, 1, -1), bsz)

        rows_l = n_rows if need_ctx else n_lat
        x_mid = _post(xs, mod, gains, ya, yb, ocf, ocb, odf, odb, zc, zd,
                      c_norm[l].reshape(1, -1), d_norm[l].reshape(1, -1),
                      w_m, w_branch[l].astype(BF16), w_out[l].astype(BF16), bsz, rows_l)
        xs = _ffn(x_mid, mod, gains, w_ff1[l].astype(BF16), w_ff2[l].astype(BF16), bsz, rows_l)
    return xs[:n_lat].reshape(bsz, SEQ, D_MODEL)
```

```python
import functools

import numpy as np
import jax
import jax.numpy as jnp
from jax import lax
from jax.experimental import pallas as pl
from jax.experimental.pallas import tpu as pltpu

D_MODEL = 1024
SEQ = 2048
DEPTH = 4
CTX_LEN = 256
GRID_W = 64
HEAD_DIM = 64
A_HEADS = 8
A_KV_HEADS = 2
A_WINDOW = 128
A_BLOCK = 128
B_HEADS = 8
NA_KH = 8
NA_KW = 16
C_HEADS = 4
C_DK = 128
D_HEADS = 4
D_DK = 64
D_GATE_RANK = 16
D_GATE_NORM = 16.0
N_BRANCH = 4
D_FF = 4 * D_MODEL
ROPE_BASE = 10000.0
EPS = 1e-6
NEG_INF = -1e30
BRANCH_W = 512

F32 = jnp.float32
BF16 = jnp.bfloat16

LANES = 128
TM = 512
TM_POST = 256
FF_CHUNK = 1024
SCAN_T = 64
VMEM_LIMIT = 52 * 1024 * 1024

GD_WIDTH = 1664

_NT = (((1,), (1,)), ((), ()))
_TN = (((0,), (0,)), ((), ()))


def _cparams(sem):
    return pltpu.CompilerParams(dimension_semantics=sem, vmem_limit_bytes=VMEM_LIMIT)


def _rms(x, gain):
    return x * lax.rsqrt(jnp.mean(x * x, axis=-1, keepdims=True) + EPS) * gain


def _norm_mod(x, gain, sc, sh):
    return _rms(x, gain) * (1.0 + sc) + sh


def _silu(x):
    return x * jax.nn.sigmoid(x)


def _log_sigmoid(x):
    return jnp.minimum(x, 0.0) - jnp.log1p(jnp.exp(-jnp.abs(x)))


def _mod_kernel(c_ref, w_ref, b_ref, o_ref):
    a = _silu(c_ref[...]).astype(BF16)
    o_ref[0] = jnp.dot(a, w_ref[0].astype(BF16), preferred_element_type=F32) + b_ref[0]


def _modulation(c_rows, w_mod, b_mod):
    rows = c_rows.shape[0]
    return pl.pallas_call(
        _mod_kernel,
        grid=(DEPTH, 6),
        in_specs=[pl.BlockSpec((rows, D_MODEL), lambda l, j: (0, 0)),
                  pl.BlockSpec((1, D_MODEL, D_MODEL), lambda l, j: (l, 0, j)),
                  pl.BlockSpec((1, 1, D_MODEL), lambda l, j: (l, 0, j))],
        out_specs=pl.BlockSpec((1, rows, D_MODEL), lambda l, j: (l, 0, j)),
        out_shape=jax.ShapeDtypeStruct((DEPTH, rows, 6 * D_MODEL), F32),
        compiler_params=_cparams(("parallel", "parallel")),
        name="modulation",
    )(c_rows, w_mod, b_mod.reshape(DEPTH, 1, 6 * D_MODEL))


def _in_proj_kernel(x_ref, mod_ref, gain_ref, w_ref, *rest, rope_slabs):
    if rope_slabs:
        cos_ref, s1_ref, s2_ref, o_ref, h_scr = rest
    else:
        o_ref, h_scr = rest

    @pl.when(pl.program_id(1) == 0)
    def _():
        m = mod_ref[0]
        h = _norm_mod(x_ref[...], gain_ref[0:1, :], m[1:2, :], m[0:1, :])
        h_scr[...] = h.astype(BF16)

    z = jnp.dot(h_scr[...], w_ref[...], preferred_element_type=F32)
    if rope_slabs:
        cos, s1, s2 = cos_ref[...], s1_ref[...], s2_ref[...]
        for s in range(rope_slabs):
            zs = z[:, s * LANES:(s + 1) * LANES]
            r = (zs * cos + pltpu.roll(zs, LANES - 16, axis=1) * s1
                 + pltpu.roll(zs, 16, axis=1) * s2)
            o_ref[:, s * LANES:(s + 1) * LANES] = r.astype(o_ref.dtype)
        rest_lo = rope_slabs * LANES
        o_ref[:, rest_lo:] = z[:, rest_lo:].astype(o_ref.dtype)
    else:
        o_ref[...] = z.astype(o_ref.dtype)


def _in_proj(xs, mod, gains, w, tn, out_dtype, bsz, rope=None):
    rows = xs.shape[0]
    width = w.shape[1]
    n_lat_tiles = bsz * SEQ // TM
    tiles_per_seq = SEQ // TM

    def mod_idx(i, j):
        return (jnp.where(i < n_lat_tiles, i // tiles_per_seq, bsz), 0, 0)

    in_specs = [pl.BlockSpec((TM, D_MODEL), lambda i, j: (i, 0)),
                pl.BlockSpec((1, 8, D_MODEL), mod_idx),
                pl.BlockSpec((4, D_MODEL), lambda i, j: (0, 0)),
                pl.BlockSpec((D_MODEL, tn), lambda i, j: (0, j))]
    args = [xs, mod, gains, w]
    slabs = 0
    if rope is not None:
        slabs = 5

        def rope_idx(i, j):
            return (jnp.where(i < n_lat_tiles, i % tiles_per_seq, tiles_per_seq), 0)

        in_specs += [pl.BlockSpec((TM, LANES), rope_idx)] * 3
        args += list(rope)
    return pl.pallas_call(
        functools.partial(_in_proj_kernel, rope_slabs=slabs),
        grid=(rows // TM, width // tn),
        in_specs=in_specs,
        out_specs=pl.BlockSpec((TM, tn), lambda i, j: (i, j)),
        out_shape=jax.ShapeDtypeStruct((rows, width), out_dtype),
        scratch_shapes=[pltpu.VMEM((TM, D_MODEL), BF16)],
        compiler_params=_cparams(("parallel", "arbitrary")),
        name="in_proj_rope" if slabs else "in_proj",
    )(*args)


def _rope_tables():
    t = jnp.arange(SEQ)
    row = (t // GRID_W).astype(F32)
    col = (t % GRID_W).astype(F32)
    half = HEAD_DIM // 2
    inv = ROPE_BASE ** (-jnp.arange(0, half, 2, dtype=F32) / half)
    lane = np.arange(LANES)
    hl = lane % HEAD_DIM
    use_row = (hl // half) == 0
    freq = hl % (half // 2)
    first = (hl % half) < (half // 2)
    ang = jnp.where(use_row[None, :], row[:, None], col[:, None]) * inv[freq][None, :]
    cos, sin = jnp.cos(ang), jnp.sin(ang)
    s1 = jnp.where(first[None, :], -sin, 0.0)
    s2 = jnp.where(first[None, :], 0.0, sin)
    ident = jnp.ones((TM, LANES), F32)
    zero = jnp.zeros((TM, LANES), F32)
    return (jnp.concatenate([cos, ident]), jnp.concatenate([s1, zero]), jnp.concatenate([s2, zero]))


def _attn_a_kernel(sink_ref, q_ref, kp_ref, kc_ref, kn_ref, vp_ref, vc_ref, vn_ref, kx_ref, vx_ref, o_ref):
    i = pl.program_id(1)
    grp = A_HEADS // A_KV_HEADS
    rows = grp * A_BLOCK
    nwin = 3 * A_BLOCK
    nkeys = nwin + CTX_LEN
    scale = HEAD_DIM ** -0.5
    r_io = lax.broadcasted_iota(jnp.int32, (rows, nkeys), 0)
    k_io = lax.broadcasted_iota(jnp.int32, (rows, nkeys), 1)
    qpos = i * A_BLOCK + (r_io & (A_BLOCK - 1))
    kpos = (i - 1) * A_BLOCK + k_io
    valid = ((jnp.abs(kpos - qpos) <= A_WINDOW) & (kpos >= 0) & (kpos < SEQ)) | (k_io >= nwin)
    head_of_row = lax.broadcasted_iota(jnp.int32, (rows, 1), 0) >> 7
    for g in range(A_KV_HEADS):
        hs = slice(g * HEAD_DIM, (g + 1) * HEAD_DIM)
        q = jnp.concatenate([q_ref[:, (g * grp + j) * HEAD_DIM:(g * grp + j + 1) * HEAD_DIM]
                             for j in range(grp)], axis=0)
        k = jnp.concatenate([kp_ref[:, hs], kc_ref[:, hs], kn_ref[:, hs], kx_ref[:, hs]], axis=0)
        v = jnp.concatenate([vp_ref[:, hs], vc_ref[:, hs], vn_ref[:, hs], vx_ref[:, hs]], axis=0)
        s = lax.dot_general(q, k, _NT, preferred_element_type=F32) * scale
        s = jnp.where(valid, s, NEG_INF)
        sink = jnp.zeros((rows, 1), F32)
        for j in range(grp):
            sink = jnp.where(head_of_row == j, sink_ref[g * grp + j], sink)
        m = jnp.maximum(jnp.max(s, axis=-1, keepdims=True), sink)
        p = jnp.exp(s - m)
        den = jnp.sum(p, axis=-1, keepdims=True) + jnp.exp(sink - m)
        o = jnp.dot(p.astype(BF16), v, preferred_element_type=F32) / den
        o = jnp.concatenate([o[j * A_BLOCK:(j + 1) * A_BLOCK] for j in range(grp)], axis=1)
        o_ref[:, g * grp * HEAD_DIM:(g + 1) * grp * HEAD_DIM] = o.astype(o_ref.dtype)


def _attn_a(za, sink, bsz):
    nb = SEQ // A_BLOCK
    qcb = A_HEADS * HEAD_DIM // LANES
    ctx_blk0 = bsz * SEQ // CTX_LEN

    def kv_spec(col, off):
        return pl.BlockSpec((A_BLOCK, LANES),
                            lambda b, i: (b * nb + jnp.clip(i + off, 0, nb - 1), col))

    return pl.pallas_call(
        _attn_a_kernel,
        grid=(bsz, nb),
        in_specs=[pl.BlockSpec(memory_space=pltpu.SMEM),
                  pl.BlockSpec((A_BLOCK, A_HEADS * HEAD_DIM), lambda b, i: (b * nb + i, 0)),
                  kv_spec(qcb, -1), kv_spec(qcb, 0), kv_spec(qcb, 1),
                  kv_spec(qcb + 1, -1), kv_spec(qcb + 1, 0), kv_spec(qcb + 1, 1),
                  pl.BlockSpec((CTX_LEN, LANES), lambda b, i: (ctx_blk0 + b, qcb)),
                  pl.BlockSpec((CTX_LEN, LANES), lambda b, i: (ctx_blk0 + b, qcb + 1))],
        out_specs=pl.BlockSpec((A_BLOCK, BRANCH_W), lambda b, i: (b * nb + i, 0)),
        out_shape=jax.ShapeDtypeStruct((bsz * SEQ, BRANCH_W), BF16),
        compiler_params=_cparams(("parallel", "parallel")),
        name="attn_window",
    )(sink, za, za, za, za, za, za, za, za, za)


def _attn_b_kernel(q_ref, k_ref, v_ref, kx_ref, vx_ref, bias_ref, o_ref):
    r = pl.program_id(1)
    rows = SEQ // GRID_W
    nwin = NA_KH * GRID_W
    r0 = jnp.clip(r - NA_KH // 2, 0, rows - NA_KH)
    start = pl.multiple_of(r0 * GRID_W, GRID_W)
    kw = k_ref[pl.ds(start, nwin), :]
    vw = v_ref[pl.ds(start, nwin), :]
    scale = HEAD_DIM ** -0.5
    outs = []
    for h in range(B_HEADS):
        hs = slice(h * HEAD_DIM, (h + 1) * HEAD_DIM)
        q = q_ref[:, hs]
        s_w = lax.dot_general(q, kw[:, hs], _NT, preferred_element_type=F32) * scale + bias_ref[0, h]
        s_c = lax.dot_general(q, kx_ref[:, hs], _NT, preferred_element_type=F32) * scale
        m = jnp.maximum(jnp.max(s_w, axis=-1, keepdims=True), jnp.max(s_c, axis=-1, keepdims=True))
        p_w = jnp.exp(s_w - m)
        p_c = jnp.exp(s_c - m)
        den = jnp.sum(p_w, axis=-1, keepdims=True) + jnp.sum(p_c, axis=-1, keepdims=True)
        o = (jnp.dot(p_w.astype(BF16), vw[:, hs], preferred_element_type=F32)
             + jnp.dot(p_c.astype(BF16), vx_ref[:, hs], preferred_element_type=F32)) / den
        outs.append(o)
    o_ref[...] = jnp.concatenate(outs, axis=1).astype(o_ref.dtype)


def _bias_expand_kernel(rb_ref, onehot_ref, o_ref):
    o_ref[...] = jnp.dot(rb_ref[...], onehot_ref[...], preferred_element_type=F32,
                         precision=lax.Precision.HIGHEST)


def _na_bias_classes(rel_bias_all):
    rows = SEQ // GRID_W
    n_dr, n_dc = 2 * NA_KH - 1, 2 * NA_KW - 1
    qc = np.arange(GRID_W)[:, None]
    kc = np.arange(GRID_W)[None, :]
    cstart = np.clip(qc - NA_KW // 2, 0, GRID_W - NA_KW)
    col_ok = (kc >= cstart) & (kc < cstart + NA_KW)
    dc_idx = np.clip(kc - qc, -(NA_KW - 1), NA_KW - 1) + (NA_KW - 1)
    onehot = (np.arange(n_dc + 1)[:, None] == dc_idx.reshape(1, -1)).astype(np.float32)
    n_tab = DEPTH * B_HEADS * n_dr
    rb = jnp.pad(rel_bias_all.astype(F32).reshape(n_tab, n_dc), ((0, 0), (0, 1)))
    dense = pl.pallas_call(
        _bias_expand_kernel,
        grid=(DEPTH,),
        in_specs=[pl.BlockSpec((n_tab // DEPTH, n_dc + 1), lambda l: (l, 0)),
                  pl.BlockSpec((n_dc + 1, GRID_W * GRID_W), lambda l: (0, 0))],
        out_specs=pl.BlockSpec((n_tab // DEPTH, GRID_W * GRID_W), lambda l: (l, 0)),
        out_shape=jax.ShapeDtypeStruct((n_tab, GRID_W * GRID_W), F32),
        compiler_params=_cparams(("parallel",)),
        name="bias_expand",
    )(rb, jnp.asarray(onehot))
    dense = dense.reshape(DEPTH, B_HEADS, n_dr, GRID_W, GRID_W)
    class_rows = [0, 1, 2, 3, NA_KH // 2, rows - 4, rows - 3, rows - 2, rows - 1]
    out = []
    for r in class_rows:
        r0 = min(max(r - NA_KH // 2, 0), rows - NA_KH)
        lo = r0 - r + (NA_KH - 1)
        b = jnp.where(col_ok[None, None, None], dense[:, :, lo:lo + NA_KH], NEG_INF)
        out.append(b.transpose(0, 1, 3, 2, 4).reshape(DEPTH, B_HEADS, GRID_W, NA_KH * GRID_W))
    return jnp.stack(out, axis=1)


def _attn_b(zb, bias_cls, bsz):
    rows = SEQ // GRID_W
    ctx_blk0 = bsz * SEQ // CTX_LEN

    def cls_idx(b, r):
        return (jnp.where(r < 4, r, jnp.where(r >= rows - 4, r - (rows - 9), 4)), 0, 0, 0)

    return pl.pallas_call(
        _attn_b_kernel,
        grid=(bsz, rows),
        in_specs=[pl.BlockSpec((GRID_W, BRANCH_W), lambda b, r: (b * rows + r, 0)),
                  pl.BlockSpec((SEQ, BRANCH_W), lambda b, r: (b, 1)),
                  pl.BlockSpec((SEQ, BRANCH_W), lambda b, r: (b, 2)),
                  pl.BlockSpec((CTX_LEN, BRANCH_W), lambda b, r: (ctx_blk0 + b, 1)),
                  pl.BlockSpec((CTX_LEN, BRANCH_W), lambda b, r: (ctx_blk0 + b, 2)),
                  pl.BlockSpec((1, B_HEADS, GRID_W, NA_KH * GRID_W), cls_idx)],
        out_specs=pl.BlockSpec((GRID_W, BRANCH_W), lambda b, r: (b * rows + r, 0)),
        out_shape=jax.ShapeDtypeStruct((bsz * SEQ, BRANCH_W), BF16),
        compiler_params=_cparams(("parallel", "arbitrary")),
        name="attn_neighbourhood",
    )(zb, zb, zb, zb, zb, bias_cls)


def _attn_ctx_kernel(sink_ref, q_ref, k_ref, v_ref, o_ref, *, n_heads, n_kv):
    grp = n_heads // n_kv
    rows = grp * CTX_LEN
    scale = HEAD_DIM ** -0.5
    head_of_row = lax.broadcasted_iota(jnp.int32, (rows, 1), 0) >> 8
    for g in range(n_kv):
        hs = slice(g * HEAD_DIM, (g + 1) * HEAD_DIM)
        q = jnp.concatenate([q_ref[:, (g * grp + j) * HEAD_DIM:(g * grp + j + 1) * HEAD_DIM]
                             for j in range(grp)], axis=0)
        s = lax.dot_general(q, k_ref[:, hs], _NT, preferred_element_type=F32) * scale
        sink = jnp.zeros((rows, 1), F32)
        for j in range(grp):
            sink = jnp.where(head_of_row == j, sink_ref[g * grp + j], sink)
        m = jnp.maximum(jnp.max(s, axis=-1, keepdims=True), sink)
        p = jnp.exp(s - m)
        den = jnp.sum(p, axis=-1, keepdims=True) + jnp.exp(sink - m)
        o = jnp.dot(p.astype(BF16), v_ref[:, hs], preferred_element_type=F32) / den
        o = jnp.concatenate([o[j * CTX_LEN:(j + 1) * CTX_LEN] for j in range(grp)], axis=1)
        o_ref[:, g * grp * HEAD_DIM:(g + 1) * grp * HEAD_DIM] = o.astype(o_ref.dtype)


def _attn_ctx(z, sink, bsz, n_heads, n_kv):
    ctx_blk0 = bsz * SEQ // CTX_LEN
    kvw = n_kv * HEAD_DIM
    kcol = n_heads * HEAD_DIM // kvw
    return pl.pallas_call(
        functools.partial(_attn_ctx_kernel, n_heads=n_heads, n_kv=n_kv),
        grid=(bsz,),
        in_specs=[pl.BlockSpec(memory_space=pltpu.SMEM),
                  pl.BlockSpec((CTX_LEN, n_heads * HEAD_DIM), lambda b: (ctx_blk0 + b, 0)),
                  pl.BlockSpec((CTX_LEN, kvw), lambda b: (ctx_blk0 + b, kcol)),
                  pl.BlockSpec((CTX_LEN, kvw), lambda b: (ctx_blk0 + b, kcol + 1))],
        out_specs=pl.BlockSpec((CTX_LEN, BRANCH_W), lambda b: (b, 0)),
        out_shape=jax.ShapeDtypeStruct((bsz * CTX_LEN, BRANCH_W), BF16),
        compiler_params=_cparams(("parallel",)),
        name="attn_ctx_%d" % n_kv,
    )(sink, z, z, z)


def _scan_chunk(q, k, v, g, s_ref, d, tri, n_heads, dk):
    t = q.shape[0]
    trif = tri.astype(F32)
    cum = jnp.dot(trif, g, preferred_element_type=F32, precision=lax.Precision.HIGHEST)
    tot = cum[t - 1:t] if d == 0 else cum[0:1]
    mid = 0.5 * tot
    qa = (q * jnp.exp(cum - mid)).astype(BF16)
    kb = (k * jnp.exp(mid - cum)).astype(BF16)
    qd = (q * jnp.exp(cum)).astype(BF16)
    kd = (k * jnp.exp(tot - cum)).astype(BF16)
    vb = v.astype(BF16)
    tot_col = lax.dot_general(g, jnp.ones((t, LANES), F32), _TN, preferred_element_type=F32,
                              precision=lax.Precision.HIGHEST)
    decay = jnp.exp(tot_col)
    outs = []
    for h in range(n_heads):
        ks = slice(h * dk, (h + 1) * dk)
        vs = slice(h * LANES, (h + 1) * LANES)
        a = lax.dot_general(qa[:, ks], kb[:, ks], _NT, preferred_element_type=F32)
        a = jnp.where(tri, a, 0.0).astype(BF16)
        state = s_ref[d, h]
        o = (jnp.dot(a, vb[:, vs], preferred_element_type=F32)
             + jnp.dot(qd[:, ks], state.astype(BF16), preferred_element_type=F32))
        outs.append(o)
        s_ref[d, h] = decay[ks, :] * state + lax.dot_general(kd[:, ks], vb[:, vs], _TN,
                                                             preferred_element_type=F32)
    return jnp.concatenate(outs, axis=1)


def _scan_kernel(cq_f, cf_f, ci_f, cq_b, cf_b, ci_b, dqk_f, dv_f, dgk_f, dqk_b, dv_b, dgk_b,
                 lb_ref, up_ref, gb_ref, ocf_ref, ocb_ref, odf_ref, odb_ref, sc_scr, sd_scr):
    @pl.when(pl.program_id(1) == 0)
    def _():
        sc_scr[...] = jnp.zeros_like(sc_scr)
        sd_scr[...] = jnp.zeros_like(sd_scr)

    t = SCAN_T
    row = lax.broadcasted_iota(jnp.int32, (t, t), 0)
    col = lax.broadcasted_iota(jnp.int32, (t, t), 1)
    lb = lb_ref[...]
    dirs = ((cq_f, cf_f, ci_f, dqk_f, dv_f, dgk_f, ocf_ref, odf_ref),
            (cq_b, cf_b, ci_b, dqk_b, dv_b, dgk_b, ocb_ref, odb_ref))
    for d, (cq, cf, ci, dqk, dv, dgk, oc_ref, od_ref) in enumerate(dirs):
        tri = (col <= row) if d == 0 else (col >= row)
        zf = cf[...]
        g = jnp.log(lb + (1.0 - lb) * jax.nn.sigmoid(zf))
        key = (1.0 - lb) * jax.nn.sigmoid(-zf)
        oc_ref[...] = _scan_chunk(_silu(cq[...]), key, ci[...], g, sc_scr, d, tri, C_HEADS, C_DK)
        zg = dgk[...][:, d * D_GATE_RANK:(d + 1) * D_GATE_RANK]
        graw = jnp.dot(zg, up_ref[d], preferred_element_type=F32,
                       precision=lax.Precision.HIGHEST) + gb_ref[d]
        g = _log_sigmoid(graw) / D_GATE_NORM
        qk = dqk[...]
        kw = D_HEADS * D_DK
        od_ref[...] = _scan_chunk(qk[:, :kw] * (D_DK ** -0.5), qk[:, kw:], dv[...], g, sd_scr, d, tri,
                                  D_HEADS, D_DK)


def _scan(zc, zd, lb, gate_up, gate_bias, bsz):
    t = SCAN_T
    n_ctx = CTX_LEN // t
    n_lat = SEQ // t
    ctx_blk0 = bsz * n_lat

    def fwd(b, c):
        return jnp.where(c < n_ctx, ctx_blk0 + b * n_ctx + c, b * n_lat + c - n_ctx)

    def bwd(b, c):
        return jnp.where(c < n_ctx, ctx_blk0 + b * n_ctx + (n_ctx - 1 - c), b * n_lat + (n_lat + n_ctx - 1 - c))

    def spec(order, width, col):
        return pl.BlockSpec((t, width), lambda b, c: (order(b, c), col))

    gk_col = (GD_WIDTH - LANES) // LANES
    in_specs = [spec(fwd, 512, 0), spec(fwd, 512, 1), spec(fwd, 512, 3),
                spec(bwd, 512, 0), spec(bwd, 512, 2), spec(bwd, 512, 3),
                spec(fwd, 512, 0), spec(fwd, 512, 1), spec(fwd, LANES, gk_col),
                spec(bwd, 512, 0), spec(bwd, 512, 1), spec(bwd, LANES, gk_col),
                pl.BlockSpec((1, 512), lambda b, c: (0, 0)),
                pl.BlockSpec((2, D_GATE_RANK, D_HEADS * D_DK), lambda b, c: (0, 0, 0)),
                pl.BlockSpec((2, 1, D_HEADS * D_DK), lambda b, c: (0, 0, 0))]
    rows = zc.shape[0]
    out = jax.ShapeDtypeStruct((rows, BRANCH_W), F32)
    return pl.pallas_call(
        _scan_kernel,
        grid=(bsz, n_ctx + n_lat),
        in_specs=in_specs,
        out_specs=[spec(fwd, 512, 0), spec(bwd, 512, 0), spec(fwd, 512, 0), spec(bwd, 512, 0)],
        out_shape=[out, out, out, out],
        scratch_shapes=[pltpu.VMEM((2, C_HEADS, C_DK, LANES), F32),
                        pltpu.VMEM((2, D_HEADS, D_DK, LANES), F32)],
        compiler_params=_cparams(("parallel", "arbitrary")),
        name="bidir_scan",
    )(zc, zc, zc, zc, zc, zc, zd, zd, zd, zd, zd, zd, lb, gate_up, gate_bias)


def _post_kernel(x_ref, mod_ref, gain_ref, ya_ref, yb_ref, ocf_ref, ocb_ref, odf_ref, odb_ref,
                 cg_ref, dg_ref, cn_ref, dn_ref, wm_ref, wb_ref, wo_ref, o_ref):
    x = x_ref[...]
    m = mod_ref[0]
    h = _norm_mod(x, gain_ref[0:1, :], m[1:2, :], m[0:1, :]).astype(BF16)

    def gated_group_norm(o, gain, gate):
        parts = [_rms(o[:, j * LANES:(j + 1) * LANES], gain) for j in range(BRANCH_W // LANES)]
        return (jnp.concatenate(parts, axis=1) * _silu(gate)).astype(BF16)

    y_c = gated_group_norm(ocf_ref[...] + ocb_ref[...], cn_ref[...], cg_ref[...])
    y_d = gated_group_norm(odf_ref[...] + odb_ref[...], dn_ref[...], dg_ref[...])
    branches = (ya_ref[...], yb_ref[...], y_c, y_d)
    acc = None
    for j, yb in enumerate(branches):
        gate = jnp.dot(h, wm_ref[:, j * D_MODEL:(j + 1) * D_MODEL], preferred_element_type=F32)
        term = jax.nn.sigmoid(gate) * jnp.dot(yb, wb_ref[j], preferred_element_type=F32)
        acc = term if acc is None else acc + term
    y = jnp.dot(acc.astype(BF16), wo_ref[...], preferred_element_type=F32)
    o_ref[...] = x + m[2:3, :] * _rms(y, gain_ref[1:2, :])


def _post(xs, mod, gains, ya, yb, ocf, ocb, odf, odb, zc, zd, c_norm, d_norm, wm, wb, wo, bsz, n_rows):
    tm = TM_POST
    n_lat_tiles = bsz * SEQ // tm
    tiles_per_seq = SEQ // tm

    def mod_idx(i):
        return (jnp.where(i < n_lat_tiles, i // tiles_per_seq, bsz), 0, 0)

    row = lambda w, col=0: pl.BlockSpec((tm, w), lambda i: (i, col))
    const = lambda shape: pl.BlockSpec(shape, lambda i: (0,) * len(shape))
    return pl.pallas_call(
        _post_kernel,
        grid=(n_rows // tm,),
        in_specs=[row(D_MODEL), pl.BlockSpec((1, 8, D_MODEL), mod_idx), const((4, D_MODEL)),
                  row(BRANCH_W), row(BRANCH_W), row(BRANCH_W), row(BRANCH_W), row(BRANCH_W), row(BRANCH_W),
                  row(BRANCH_W, 4), row(BRANCH_W, 2),
                  const((1, LANES)), const((1, LANES)),
                  const((D_MODEL, N_BRANCH * D_MODEL)), const((N_BRANCH, BRANCH_W, D_MODEL)),
                  const((D_MODEL, D_MODEL))],
        out_specs=row(D_MODEL),
        out_shape=jax.ShapeDtypeStruct((n_rows, D_MODEL), F32),
        compiler_params=_cparams(("parallel",)),
        name="merge_out",
    )(xs, mod, gains, ya, yb, ocf, ocb, odf, odb, zc, zd, c_norm, d_norm, wm, wb, wo)


def _ffn_kernel(x_ref, mod_ref, gain_ref, w1_ref, w2_ref, o_ref, h_scr, acc_scr):
    k = pl.program_id(1)

    @pl.when(k == 0)
    def _():
        m = mod_ref[0]
        h_scr[...] = _norm_mod(x_ref[...], gain_ref[2:3, :], m[4:5, :], m[3:4, :]).astype(BF16)
        acc_scr[...] = jnp.zeros_like(acc_scr)

    u = jnp.dot(h_scr[...], w1_ref[...], preferred_element_type=F32)
    u = jnp.square(jnp.maximum(u, 0.0)).astype(BF16)
    acc_scr[...] += jnp.dot(u, w2_ref[...], preferred_element_type=F32)

    @pl.when(k == pl.num_programs(1) - 1)
    def _():
        m = mod_ref[0]
        o_ref[...] = x_ref[...] + m[5:6, :] * _rms(acc_scr[...], gain_ref[3:4, :])


def _ffn(xs, mod, gains, w1, w2, bsz, n_rows):
    n_lat_tiles = bsz * SEQ // TM
    tiles_per_seq = SEQ // TM

    def mod_idx(i, k):
        return (jnp.where(i < n_lat_tiles, i // tiles_per_seq, bsz), 0, 0)

    return pl.pallas_call(
        _ffn_kernel,
        grid=(n_rows // TM, D_FF // FF_CHUNK),
        in_specs=[pl.BlockSpec((TM, D_MODEL), lambda i, k: (i, 0)),
                  pl.BlockSpec((1, 8, D_MODEL), mod_idx),
                  pl.BlockSpec((4, D_MODEL), lambda i, k: (0, 0)),
                  pl.BlockSpec((D_MODEL, FF_CHUNK), lambda i, k: (0, k)),
                  pl.BlockSpec((FF_CHUNK, D_MODEL), lambda i, k: (k, 0))],
        out_specs=pl.BlockSpec((TM, D_MODEL), lambda i, k: (i, 0)),
        out_shape=jax.ShapeDtypeStruct((n_rows, D_MODEL), F32),
        scratch_shapes=[pltpu.VMEM((TM, D_MODEL), BF16), pltpu.VMEM((TM, D_MODEL), F32)],
        compiler_params=_cparams(("parallel", "arbitrary")),
        name="mlp",
    )(xs, mod, gains, w1, w2)


def _split_w_in(w_in_l):
    a_w = A_HEADS * HEAD_DIM + 2 * A_KV_HEADS * HEAD_DIM
    b_w = 3 * B_HEADS * HEAD_DIM
    c_w = 5 * C_HEADS * C_DK
    o_a, o_b, o_c = 0, a_w, a_w + b_w
    o_d = o_c + c_w
    kw = D_HEADS * D_DK
    d_q = w_in_l[:, o_d:o_d + kw]
    d_k = w_in_l[:, o_d + kw:o_d + 2 * kw]
    d_v = w_in_l[:, o_d + 2 * kw:o_d + 2 * kw + 512]
    o_gk = o_d + 2 * kw + 512
    d_gk = w_in_l[:, o_gk:o_gk + 2 * D_GATE_RANK]
    d_g = w_in_l[:, o_gk + 2 * D_GATE_RANK:o_gk + 2 * D_GATE_RANK + 512]
    o_m = o_gk + 2 * D_GATE_RANK + 512
    pad = jnp.zeros((D_MODEL, LANES - 2 * D_GATE_RANK), w_in_l.dtype)
    g_d = jnp.concatenate([d_q, d_k, d_v, d_g, d_gk, pad], axis=1)
    groups = (w_in_l[:, o_a:o_b], w_in_l[:, o_b:o_c], w_in_l[:, o_c:o_d], g_d, w_in_l[:, o_m:])
    return tuple(g.astype(BF16) for g in groups)


def kernel(x, c, ctx, c_ctx, w_mod, b_mod, norm_gains, w_in, a_sink, b_rel_bias, c_lower_bounds, c_norm,
           d_gate_up, d_gate_bias, d_norm, w_branch, w_out, w_ff1, w_ff2):
    bsz = x.shape[0]
    n_lat = bsz * SEQ
    n_rows = n_lat + bsz * CTX_LEN

    lb_soft = jax.nn.softmax(c_lower_bounds.astype(F32), axis=0)
    lb_all = jnp.cumsum(lb_soft, axis=0) - lb_soft[0:1]

    c_rows = jnp.concatenate([c, c_ctx[None, :], jnp.zeros((-(bsz + 1) % 8, D_MODEL), F32)], axis=0)
    mod_all = _modulation(c_rows, w_mod, b_mod)[:, :bsz + 1]
    mod_all = mod_all.reshape(DEPTH, bsz + 1, 6, D_MODEL)
    mod_all = jnp.pad(mod_all, ((0, 0), (0, 0), (0, 2), (0, 0)))

    rope = _rope_tables()
    bias_cls = _na_bias_classes(b_rel_bias)
    no_sink = jnp.full((B_HEADS,), NEG_INF, F32)
    xs = jnp.concatenate([x.reshape(n_lat, D_MODEL), ctx.reshape(bsz * CTX_LEN, D_MODEL)], axis=0)

    for l in range(DEPTH):
        need_ctx = l < DEPTH - 1
        mod, gains = mod_all[l], norm_gains[l]
        w_a, w_b, w_c, w_d, w_m = _split_w_in(w_in[l])
        za = _in_proj(xs, mod, gains, w_a, w_a.shape[1], BF16, bsz, rope=rope)
        zb = _in_proj(xs, mod, gains, w_b, 512, BF16, bsz)
        zc = _in_proj(xs, mod, gains, w_c, 512, F32, bsz)
        zd = _in_proj(xs, mod, gains, w_d, GD_WIDTH, F32, bsz)

        ya = _attn_a(za, a_sink[l], bsz)
        yb = _attn_b(zb, bias_cls[l], bsz)
        if need_ctx:
            ya = jnp.concatenate([ya, _attn_ctx(za, a_sink[l], bsz, A_HEADS, A_KV_HEADS)], axis=0)
            yb = jnp.concatenate([yb, _attn_ctx(zb, no_sink, bsz, B_HEADS, B_HEADS)], axis=0)
        ocf, ocb, odf, odb = _scan(zc, zd, lb_all[l].reshape(1, -1), d_gate_up[l],
                                   d_gate_bias[l].reshape(2, 1, -1), bsz)

        rows_l = n_rows if need_ctx else n_lat
        x_mid = _post(xs, mod, gains, ya, yb, ocf, ocb, odf, odb, zc, zd,
                      c_norm[l].reshape(1, -1), d_norm[l].reshape(1, -1),
                      w_m, w_branch[l].astype(BF16), w_out[l].astype(BF16), bsz, rows_l)
        xs = _ffn(x_mid, mod, gains, w_ff1[l].astype(BF16), w_ff2[l].astype(BF16), bsz, rows_l)
    return xs[:n_lat].reshape(bsz, SEQ, D_MODEL)
```

```python
import functools

import numpy as np
import jax
import jax.numpy as jnp
from jax import lax
from jax.experimental import pallas as pl
from jax.experimental.pallas import tpu as pltpu

D_MODEL = 1024
SEQ = 2048
DEPTH = 4
CTX_LEN = 256
GRID_W = 64
HEAD_DIM = 64
A_HEADS = 8
A_KV_HEADS = 2
A_WINDOW = 128
A_BLOCK = 128
B_HEADS = 8
NA_KH = 8
NA_KW = 16
C_HEADS = 4
C_DK = 128
D_HEADS = 4
D_DK = 64
D_GATE_RANK = 16
D_GATE_NORM = 16.0
N_BRANCH = 4
D_FF = 4 * D_MODEL
ROPE_BASE = 10000.0
EPS = 1e-6
NEG_INF = -1e30
BRANCH_W = 512

F32 = jnp.float32
BF16 = jnp.bfloat16

LANES = 128
TM = 512
TM_POST = 256
FF_CHUNK = 1024
SCAN_T = 64
VMEM_LIMIT = 52 * 1024 * 1024

GD_WIDTH = 1664

_NT = (((1,), (1,)), ((), ()))
_TN = (((0,), (0,)), ((), ()))


def _cparams(sem):
    return pltpu.CompilerParams(dimension_semantics=sem, vmem_limit_bytes=VMEM_LIMIT)


def _rms(x, gain):
    return x * lax.rsqrt(jnp.mean(x * x, axis=-1, keepdims=True) + EPS) * gain


def _norm_mod(x, gain, sc, sh):
    return _rms(x, gain) * (1.0 + sc) + sh


def _silu(x):
    return x * jax.nn.sigmoid(x)


def _log_sigmoid(x):
    return jnp.minimum(x, 0.0) - jnp.log1p(jnp.exp(-jnp.abs(x)))


def _mod_kernel(c_ref, w_ref, b_ref, o_ref):
    a = _silu(c_ref[...]).astype(BF16)
    o_ref[0] = jnp.dot(a, w_ref[0].astype(BF16), preferred_element_type=F32) + b_ref[0]


def _modulation(c_rows, w_mod, b_mod):
    rows = c_rows.shape[0]
    return pl.pallas_call(
        _mod_kernel,
        grid=(DEPTH, 6),
        in_specs=[pl.BlockSpec((rows, D_MODEL), lambda l, j: (0, 0)),
                  pl.BlockSpec((1, D_MODEL, D_MODEL), lambda l, j: (l, 0, j)),
                  pl.BlockSpec((1, 1, D_MODEL), lambda l, j: (l, 0, j))],
        out_specs=pl.BlockSpec((1, rows, D_MODEL), lambda l, j: (l, 0, j)),
        out_shape=jax.ShapeDtypeStruct((DEPTH, rows, 6 * D_MODEL), F32),
        compiler_params=_cparams(("parallel", "parallel")),
        name="modulation",
    )(c_rows, w_mod, b_mod.reshape(DEPTH, 1, 6 * D_MODEL))


def _in_proj_kernel(x_ref, mod_ref, gain_ref, w_ref, *rest, rope_slabs):
    if rope_slabs:
        cos_ref, s1_ref, s2_ref, o_ref, h_scr = rest
    else:
        o_ref, h_scr = rest

    @pl.when(pl.program_id(1) == 0)
    def _():
        m = mod_ref[0]
        h = _norm_mod(x_ref[...], gain_ref[0:1, :], m[1:2, :], m[0:1, :])
        h_scr[...] = h.astype(BF16)

    z = jnp.dot(h_scr[...], w_ref[...], preferred_element_type=F32)
    if rope_slabs:
        cos, s1, s2 = cos_ref[...], s1_ref[...], s2_ref[...]
        for s in range(rope_slabs):
            zs = z[:, s * LANES:(s + 1) * LANES]
            r = (zs * cos + pltpu.roll(zs, LANES - 16, axis=1) * s1
                 + pltpu.roll(zs, 16, axis=1) * s2)
            o_ref[:, s * LANES:(s + 1) * LANES] = r.astype(o_ref.dtype)
        rest_lo = rope_slabs * LANES
        o_ref[:, rest_lo:] = z[:, rest_lo:].astype(o_ref.dtype)
    else:
        o_ref[...] = z.astype(o_ref.dtype)


def _in_proj(xs, mod, gains, w, tn, out_dtype, bsz, rope=None):
    rows = xs.shape[0]
    width = w.shape[1]
    n_lat_tiles = bsz * SEQ // TM
    tiles_per_seq = SEQ // TM

    def mod_idx(i, j):
        return (jnp.where(i < n_lat_tiles, i // tiles_per_seq, bsz), 0, 0)

    in_specs = [pl.BlockSpec((TM, D_MODEL), lambda i, j: (i, 0)),
                pl.BlockSpec((1, 8, D_MODEL), mod_idx),
                pl.BlockSpec((4, D_MODEL), lambda i, j: (0, 0)),
                pl.BlockSpec((D_MODEL, tn), lambda i, j: (0, j))]
    args = [xs, mod, gains, w]
    slabs = 0
    if rope is not None:
        slabs = 5

        def rope_idx(i, j):
            return (jnp.where(i < n_lat_tiles, i % tiles_per_seq, tiles_per_seq), 0)

        in_specs += [pl.BlockSpec((TM, LANES), rope_idx)] * 3
        args += list(rope)
    return pl.pallas_call(
        functools.partial(_in_proj_kernel, rope_slabs=slabs),
        grid=(rows // TM, width // tn),
        in_specs=in_specs,
        out_specs=pl.BlockSpec((TM, tn), lambda i, j: (i, j)),
        out_shape=jax.ShapeDtypeStruct((rows, width), out_dtype),
        scratch_shapes=[pltpu.VMEM((TM, D_MODEL), BF16)],
        compiler_params=_cparams(("parallel", "arbitrary")),
        name="in_proj_rope" if slabs else "in_proj",
    )(*args)


def _rope_tables():
    t = jnp.arange(SEQ)
    row = (t // GRID_W).astype(F32)
    col = (t % GRID_W).astype(F32)
    half = HEAD_DIM // 2
    inv = ROPE_BASE ** (-jnp.arange(0, half, 2, dtype=F32) / half)
    lane = np.arange(LANES)
    hl = lane % HEAD_DIM
    use_row = (hl // half) == 0
    freq = hl % (half // 2)
    first = (hl % half) < (half // 2)
    ang = jnp.where(use_row[None, :], row[:, None], col[:, None]) * inv[freq][None, :]
    cos, sin = jnp.cos(ang), jnp.sin(ang)
    s1 = jnp.where(first[None, :], -sin, 0.0)
    s2 = jnp.where(first[None, :], 0.0, sin)
    ident = jnp.ones((TM, LANES), F32)
    zero = jnp.zeros((TM, LANES), F32)
    return (jnp.concatenate([cos, ident]), jnp.concatenate([s1, zero]), jnp.concatenate([s2, zero]))


def _attn_a_kernel(sink_ref, q_ref, kp_ref, kc_ref, kn_ref, vp_ref, vc_ref, vn_ref, kx_ref, vx_ref, mask_ref,
                   o_ref):
    grp = A_HEADS // A_KV_HEADS
    rows = grp * A_BLOCK
    scale = HEAD_DIM ** -0.5
    mask = jnp.concatenate([mask_ref[0]] * grp, axis=0)
    head_of_row = lax.broadcasted_iota(jnp.int32, (rows, 1), 0) >> 7

    def scores(g):
        hs = slice(g * HEAD_DIM, (g + 1) * HEAD_DIM)
        q = jnp.concatenate([q_ref[:, (g * grp + j) * HEAD_DIM:(g * grp + j + 1) * HEAD_DIM]
                             for j in range(grp)], axis=0) * scale
        k = jnp.concatenate([kp_ref[:, hs], kc_ref[:, hs], kn_ref[:, hs], kx_ref[:, hs]], axis=0)
        return lax.dot_general(q, k, _NT, preferred_element_type=F32)

    all_scores = [scores(g) for g in range(A_KV_HEADS)]
    for g in range(A_KV_HEADS):
        hs = slice(g * HEAD_DIM, (g + 1) * HEAD_DIM)
        v = jnp.concatenate([vp_ref[:, hs], vc_ref[:, hs], vn_ref[:, hs], vx_ref[:, hs]], axis=0)
        s = all_scores[g] + mask
        sink = jnp.zeros((rows, 1), F32)
        for j in range(grp):
            sink = jnp.where(head_of_row == j, sink_ref[g * grp + j], sink)
        m = jnp.maximum(jnp.max(s, axis=-1, keepdims=True), sink)
        p = jnp.exp(s - m)
        den = jnp.sum(p, axis=-1, keepdims=True) + jnp.exp(sink - m)
        o = jnp.dot(p.astype(BF16), v, preferred_element_type=F32) / den
        o = jnp.concatenate([o[j * A_BLOCK:(j + 1) * A_BLOCK] for j in range(grp)], axis=1)
        o_ref[:, g * grp * HEAD_DIM:(g + 1) * grp * HEAD_DIM] = o.astype(o_ref.dtype)


def _window_masks():
    nb = SEQ // A_BLOCK
    qi = np.arange(A_BLOCK)[:, None]
    kj = np.arange(3 * A_BLOCK)[None, :]
    out = []
    for i in (0, 1, nb - 1):
        qpos = i * A_BLOCK + qi
        kpos = (i - 1) * A_BLOCK + kj
        ok = (np.abs(kpos - qpos) <= A_WINDOW) & (kpos >= 0) & (kpos < SEQ)
        ok = np.concatenate([ok, np.ones((A_BLOCK, CTX_LEN), bool)], axis=1)
        out.append(np.where(ok, 0.0, NEG_INF).astype(np.float32))
    return jnp.asarray(np.stack(out))


def _attn_a(za, sink, bsz):
    nb = SEQ // A_BLOCK
    qcb = A_HEADS * HEAD_DIM // LANES
    ctx_blk0 = bsz * SEQ // CTX_LEN
    nkeys = 3 * A_BLOCK + CTX_LEN

    def mask_idx(b, i):
        return (jnp.where(i == 0, 0, jnp.where(i == nb - 1, 2, 1)), 0, 0)

    def kv_spec(col, off):
        return pl.BlockSpec((A_BLOCK, LANES),
                            lambda b, i: (b * nb + jnp.clip(i + off, 0, nb - 1), col))

    return pl.pallas_call(
        _attn_a_kernel,
        grid=(bsz, nb),
        in_specs=[pl.BlockSpec(memory_space=pltpu.SMEM),
                  pl.BlockSpec((A_BLOCK, A_HEADS * HEAD_DIM), lambda b, i: (b * nb + i, 0)),
                  kv_spec(qcb, -1), kv_spec(qcb, 0), kv_spec(qcb, 1),
                  kv_spec(qcb + 1, -1), kv_spec(qcb + 1, 0), kv_spec(qcb + 1, 1),
                  pl.BlockSpec((CTX_LEN, LANES), lambda b, i: (ctx_blk0 + b, qcb)),
                  pl.BlockSpec((CTX_LEN, LANES), lambda b, i: (ctx_blk0 + b, qcb + 1)),
                  pl.BlockSpec((1, A_BLOCK, nkeys), mask_idx)],
        out_specs=pl.BlockSpec((A_BLOCK, BRANCH_W), lambda b, i: (b * nb + i, 0)),
        out_shape=jax.ShapeDtypeStruct((bsz * SEQ, BRANCH_W), BF16),
        compiler_params=_cparams(("parallel", "parallel")),
        name="attn_window",
    )(sink, za, za, za, za, za, za, za, za, za, _window_masks())


NA_QROWS = 4
NA_KROWS = 12
NA_QTOK = NA_QROWS * GRID_W
NA_KEYS = NA_KROWS * GRID_W + CTX_LEN


def _attn_b_kernel(q_ref, k0_ref, k1_ref, k2_ref, v0_ref, v1_ref, v2_ref, kx_ref, vx_ref, bias_ref, o_ref):
    scale = HEAD_DIM ** -0.5

    def scores(h):
        hs = slice(h * HEAD_DIM, (h + 1) * HEAD_DIM)
        q = q_ref[:, hs] * scale
        k = jnp.concatenate([k0_ref[:, hs], k1_ref[:, hs], k2_ref[:, hs], kx_ref[:, hs]], axis=0)
        return lax.dot_general(q, k, _NT, preferred_element_type=F32)

    outs = []
    s_next = scores(0)
    for h in range(B_HEADS):
        hs = slice(h * HEAD_DIM, (h + 1) * HEAD_DIM)
        s = s_next + bias_ref[0, h]
        if h + 1 < B_HEADS:
            s_next = scores(h + 1)
        v = jnp.concatenate([v0_ref[:, hs], v1_ref[:, hs], v2_ref[:, hs], vx_ref[:, hs]], axis=0)
        m = jnp.max(s, axis=-1, keepdims=True)
        p = jnp.exp(s - m)
        den = jnp.sum(p, axis=-1, keepdims=True)
        outs.append(jnp.dot(p.astype(BF16), v, preferred_element_type=F32) / den)
    o_ref[...] = jnp.concatenate(outs, axis=1).astype(o_ref.dtype)


def _bias_expand_kernel(rb_ref, onehot_ref, o_ref):
    o_ref[...] = jnp.dot(rb_ref[...], onehot_ref[...], preferred_element_type=F32,
                         precision=lax.Precision.HIGHEST)


def _na_bias_classes(rel_bias_all):
    rows = SEQ // GRID_W
    n_dr, n_dc = 2 * NA_KH - 1, 2 * NA_KW - 1
    qc = np.arange(GRID_W)[:, None]
    kc = np.arange(GRID_W)[None, :]
    cstart = np.clip(qc - NA_KW // 2, 0, GRID_W - NA_KW)
    col_ok = (kc >= cstart) & (kc < cstart + NA_KW)
    dc_idx = np.clip(kc - qc, -(NA_KW - 1), NA_KW - 1) + (NA_KW - 1)
    onehot = (np.arange(n_dc + 1)[:, None] == dc_idx.reshape(1, -1)).astype(np.float32)
    n_tab = DEPTH * B_HEADS * n_dr
    rb = jnp.pad(rel_bias_all.astype(F32).reshape(n_tab, n_dc), ((0, 0), (0, 1)))
    dense = pl.pallas_call(
        _bias_expand_kernel,
        grid=(DEPTH,),
        in_specs=[pl.BlockSpec((n_tab // DEPTH, n_dc + 1), lambda l: (l, 0)),
                  pl.BlockSpec((n_dc + 1, GRID_W * GRID_W), lambda l: (0, 0))],
        out_specs=pl.BlockSpec((n_tab // DEPTH, GRID_W * GRID_W), lambda l: (l, 0)),
        out_shape=jax.ShapeDtypeStruct((n_tab, GRID_W * GRID_W), F32),
        compiler_params=_cparams(("parallel",)),
        name="bias_expand",
    )(rb, jnp.asarray(onehot))
    dense = dense.reshape(DEPTH, B_HEADS, n_dr, GRID_W, GRID_W)
    masked = jnp.where(col_ok[None, None, None], dense, NEG_INF)
    n_groups = rows // NA_QROWS
    out = []
    for g in (0, 1, n_groups - 1):
        base = _na_key_base(g)
        per_row = []
        for lr in range(NA_QROWS):
            r = g * NA_QROWS + lr
            r0 = min(max(r - NA_KH // 2, 0), rows - NA_KH)
            dr_lo = r0 - r + (NA_KH - 1)
            before = r0 - base
            piece = jnp.pad(masked[:, :, dr_lo:dr_lo + NA_KH],
                            ((0, 0), (0, 0), (before, NA_KROWS - NA_KH - before), (0, 0), (0, 0)),
                            constant_values=NEG_INF)
            per_row.append(piece.transpose(0, 1, 3, 2, 4).reshape(DEPTH, B_HEADS, GRID_W, NA_KROWS * GRID_W))
        cls = jnp.concatenate(per_row, axis=2)
        out.append(jnp.pad(cls, ((0, 0), (0, 0), (0, 0), (0, CTX_LEN))))
    return jnp.stack(out, axis=1)


def _na_key_base(g):
    rows = SEQ // GRID_W
    return min(max(g * NA_QROWS - NA_KH // 2, 0), rows - NA_KROWS)


def _attn_b(zb, bias_cls, bsz):
    n_groups = SEQ // NA_QTOK
    ctx_blk0 = bsz * SEQ // CTX_LEN
    max_base_blk = _na_key_base(n_groups - 1) // NA_QROWS

    def cls_idx(g, b):
        return (jnp.where(g == 0, 0, jnp.where(g == n_groups - 1, 2, 1)), 0, 0, 0)

    def kv_spec(col, j):
        return pl.BlockSpec((NA_QTOK, BRANCH_W),
                            lambda g, b: (b * n_groups + jnp.clip(g - 1, 0, max_base_blk) + j, col))

    return pl.pallas_call(
        _attn_b_kernel,
        grid=(n_groups, bsz),
        in_specs=[pl.BlockSpec((NA_QTOK, BRANCH_W), lambda g, b: (b * n_groups + g, 0)),
                  kv_spec(1, 0), kv_spec(1, 1), kv_spec(1, 2),
                  kv_spec(2, 0), kv_spec(2, 1), kv_spec(2, 2),
                  pl.BlockSpec((CTX_LEN, BRANCH_W), lambda g, b: (ctx_blk0 + b, 1)),
                  pl.BlockSpec((CTX_LEN, BRANCH_W), lambda g, b: (ctx_blk0 + b, 2)),
                  pl.BlockSpec((1, B_HEADS, NA_QTOK, NA_KEYS), cls_idx)],
        out_specs=pl.BlockSpec((NA_QTOK, BRANCH_W), lambda g, b: (b * n_groups + g, 0)),
        out_shape=jax.ShapeDtypeStruct((bsz * SEQ, BRANCH_W), BF16),
        compiler_params=_cparams(("arbitrary", "arbitrary")),
        name="attn_neighbourhood",
    )(zb, zb, zb, zb, zb, zb, zb, zb, zb, bias_cls)


def _attn_ctx_kernel(sink_ref, q_ref, k_ref, v_ref, o_ref, *, n_heads, n_kv):
    grp = n_heads // n_kv
    rows = grp * CTX_LEN
    scale = HEAD_DIM ** -0.5
    head_of_row = lax.broadcasted_iota(jnp.int32, (rows, 1), 0) >> 8
    for g in range(n_kv):
        hs = slice(g * HEAD_DIM, (g + 1) * HEAD_DIM)
        q = jnp.concatenate([q_ref[:, (g * grp + j) * HEAD_DIM:(g * grp + j + 1) * HEAD_DIM]
                             for j in range(grp)], axis=0)
        s = lax.dot_general(q, k_ref[:, hs], _NT, preferred_element_type=F32) * scale
        sink = jnp.zeros((rows, 1), F32)
        for j in range(grp):
            sink = jnp.where(head_of_row == j, sink_ref[g * grp + j], sink)
        m = jnp.maximum(jnp.max(s, axis=-1, keepdims=True), sink)
        p = jnp.exp(s - m)
        den = jnp.sum(p, axis=-1, keepdims=True) + jnp.exp(sink - m)
        o = jnp.dot(p.astype(BF16), v_ref[:, hs], preferred_element_type=F32) / den
        o = jnp.concatenate([o[j * CTX_LEN:(j + 1) * CTX_LEN] for j in range(grp)], axis=1)
        o_ref[:, g * grp * HEAD_DIM:(g + 1) * grp * HEAD_DIM] = o.astype(o_ref.dtype)


def _attn_ctx(z, sink, bsz, n_heads, n_kv):
    ctx_blk0 = bsz * SEQ // CTX_LEN
    kvw = n_kv * HEAD_DIM
    kcol = n_heads * HEAD_DIM // kvw
    return pl.pallas_call(
        functools.partial(_attn_ctx_kernel, n_heads=n_heads, n_kv=n_kv),
        grid=(bsz,),
        in_specs=[pl.BlockSpec(memory_space=pltpu.SMEM),
                  pl.BlockSpec((CTX_LEN, n_heads * HEAD_DIM), lambda b: (ctx_blk0 + b, 0)),
                  pl.BlockSpec((CTX_LEN, kvw), lambda b: (ctx_blk0 + b, kcol)),
                  pl.BlockSpec((CTX_LEN, kvw), lambda b: (ctx_blk0 + b, kcol + 1))],
        out_specs=pl.BlockSpec((CTX_LEN, BRANCH_W), lambda b: (b, 0)),
        out_shape=jax.ShapeDtypeStruct((bsz * CTX_LEN, BRANCH_W), BF16),
        compiler_params=_cparams(("parallel",)),
        name="attn_ctx_%d" % n_kv,
    )(sink, z, z, z)


def _scan_chunks(chains, tris):
    tribs = [tri.astype(BF16) for tri in tris]
    cums = []
    for (q, k, v, g, state, d, n_heads, dk) in chains:
        g_hi = g.astype(BF16)
        rem = g - g_hi.astype(F32)
        g_mid = rem.astype(BF16)
        g_lo = (rem - g_mid.astype(F32)).astype(BF16)
        cums.append(jnp.dot(tribs[d], g_hi, preferred_element_type=F32)
                    + jnp.dot(tribs[d], g_mid, preferred_element_type=F32)
                    + jnp.dot(tribs[d], g_lo, preferred_element_type=F32))
    prepped = []
    for (q, k, v, g, state, d, n_heads, dk), cum in zip(chains, cums):
        t = q.shape[0]
        tot = cum[t - 1:t] if d == 0 else cum[0:1]
        mid = 0.5 * tot
        e_mid = jnp.exp(mid)
        qa = q * jnp.exp(cum - mid)
        kb = k * jnp.exp(mid - cum)
        qd = (qa * e_mid).astype(BF16)
        kd = (kb * e_mid).astype(BF16)
        prepped.append((qa.astype(BF16), kb.astype(BF16), qd, kd, v.astype(BF16),
                        state * jnp.exp(tot), state.astype(BF16)))
    partial = []
    for (q, k, v, g, state, d, n_heads, dk), (qa, kb, qd, kd, vb, decayed, state_b) in zip(chains, prepped):
        per_head = []
        for h in range(n_heads):
            ks = slice(h * dk, (h + 1) * dk)
            vs = slice(h * LANES, (h + 1) * LANES)
            a = lax.dot_general(qa[:, ks], kb[:, ks], _NT, preferred_element_type=F32)
            carry = lax.dot_general(qd[:, ks], state_b[:, ks], _NT, preferred_element_type=F32)
            upd = lax.dot_general(vb[:, vs], kd[:, ks], _TN, preferred_element_type=F32)
            per_head.append((a, carry, decayed[:, ks] + upd))
        partial.append(per_head)
    results = []
    for (q, k, v, g, state, d, n_heads, dk), (qa, kb, qd, kd, vb, decayed, state_b), per_head in zip(
            chains, prepped, partial):
        outs = []
        for h, (a, carry, new_s) in enumerate(per_head):
            a = jnp.where(tris[d], a, 0.0).astype(BF16)
            outs.append(jnp.dot(a, vb[:, h * LANES:(h + 1) * LANES], preferred_element_type=F32) + carry)
        results.append((jnp.concatenate(outs, axis=1), jnp.concatenate([p[2] for p in per_head], axis=1)))
    return results


def _scan_kernel(cq_f, cf_f, ci_f, cq_b, cf_b, ci_b, dqk_f, dv_f, dgk_f, dqk_b, dv_b, dgk_b,
                 lb_ref, up_ref, gb_ref, ocf_ref, ocb_ref, odf_ref, odb_ref, scf_scr, scb_scr, sdf_scr, sdb_scr):
    scratches = (scf_scr, scb_scr, sdf_scr, sdb_scr)

    @pl.when(pl.program_id(1) == 0)
    def _():
        for s in scratches:
            s[...] = jnp.zeros_like(s)

    states = [s[...] for s in scratches]
    t = SCAN_T
    row = lax.broadcasted_iota(jnp.int32, (t, t), 0)
    col = lax.broadcasted_iota(jnp.int32, (t, t), 1)
    lb = lb_ref[...]
    kw = D_HEADS * D_DK
    dirs = ((cq_f, cf_f, ci_f, dqk_f, dv_f, dgk_f), (cq_b, cf_b, ci_b, dqk_b, dv_b, dgk_b))
    tris = (col <= row, col >= row)
    graws = [jnp.dot(dgk[...][:, d * D_GATE_RANK:(d + 1) * D_GATE_RANK], up_ref[d], preferred_element_type=F32,
                     precision=lax.Precision.HIGHEST) + gb_ref[d]
             for d, (cq, cf, ci, dqk, dv, dgk) in enumerate(dirs)]
    chains = []
    for d, (cq, cf, ci, dqk, dv, dgk) in enumerate(dirs):
        forget = jax.nn.sigmoid(cf[...])
        g = jnp.log(lb + (1.0 - lb) * forget)
        key = (1.0 - lb) * (1.0 - forget)
        chains.append((_silu(cq[...]), key, ci[...], g, states[d], d, C_HEADS, C_DK))
        g = _log_sigmoid(graws[d]) / D_GATE_NORM
        qk = dqk[...]
        chains.append((qk[:, :kw] * (D_DK ** -0.5), qk[:, kw:], dv[...], g, states[2 + d], d, D_HEADS, D_DK))
    (ocf, scf), (odf, sdf), (ocb, scb), (odb, sdb) = _scan_chunks(chains, tris)
    for ref, val in ((ocf_ref, ocf), (ocb_ref, ocb), (odf_ref, odf), (odb_ref, odb),
                     (scf_scr, scf), (scb_scr, scb), (sdf_scr, sdf), (sdb_scr, sdb)):
        ref[...] = val


def _scan(zc, zd, lb, gate_up, gate_bias, bsz):
    t = SCAN_T
    n_ctx = CTX_LEN // t
    n_lat = SEQ // t
    ctx_blk0 = bsz * n_lat

    def fwd(b, c):
        return jnp.where(c < n_ctx, ctx_blk0 + b * n_ctx + c, b * n_lat + c - n_ctx)

    def bwd(b, c):
        return jnp.where(c < n_ctx, ctx_blk0 + b * n_ctx + (n_ctx - 1 - c), b * n_lat + (n_lat + n_ctx - 1 - c))

    def spec(order, width, col):
        return pl.BlockSpec((t, width), lambda b, c: (order(b, c), col))

    gk_col = (GD_WIDTH - LANES) // LANES
    in_specs = [spec(fwd, 512, 0), spec(fwd, 512, 1), spec(fwd, 512, 3),
                spec(bwd, 512, 0), spec(bwd, 512, 2), spec(bwd, 512, 3),
                spec(fwd, 512, 0), spec(fwd, 512, 1), spec(fwd, LANES, gk_col),
                spec(bwd, 512, 0), spec(bwd, 512, 1), spec(bwd, LANES, gk_col),
                pl.BlockSpec((1, 512), lambda b, c: (0, 0)),
                pl.BlockSpec((2, D_GATE_RANK, D_HEADS * D_DK), lambda b, c: (0, 0, 0)),
                pl.BlockSpec((2, 1, D_HEADS * D_DK), lambda b, c: (0, 0, 0))]
    rows = zc.shape[0]
    out = jax.ShapeDtypeStruct((rows, BRANCH_W), F32)
    return pl.pallas_call(
        _scan_kernel,
        grid=(bsz, n_ctx + n_lat),
        in_specs=in_specs,
        out_specs=[spec(fwd, 512, 0), spec(bwd, 512, 0), spec(fwd, 512, 0), spec(bwd, 512, 0)],
        out_shape=[out, out, out, out],
        scratch_shapes=[pltpu.VMEM((LANES, C_HEADS * C_DK), F32), pltpu.VMEM((LANES, C_HEADS * C_DK), F32),
                        pltpu.VMEM((LANES, D_HEADS * D_DK), F32), pltpu.VMEM((LANES, D_HEADS * D_DK), F32)],
        compiler_params=_cparams(("parallel", "arbitrary")),
        name="bidir_scan",
    )(zc, zc, zc, zc, zc, zc, zd, zd, zd, zd, zd, zd, lb, gate_up, gate_bias)


def _post_kernel(x_ref, mod_ref, gain_ref, ya_ref, yb_ref, ocf_ref, ocb_ref, odf_ref, odb_ref,
                 cg_ref, dg_ref, cn_ref, dn_ref, wm_ref, wb_ref, wo_ref, o_ref):
    x = x_ref[...]
    m = mod_ref[0]
    h = _norm_mod(x, gain_ref[0:1, :], m[1:2, :], m[0:1, :]).astype(BF16)

    def gated_group_norm(o, gain, gate):
        parts = [_rms(o[:, j * LANES:(j + 1) * LANES], gain) for j in range(BRANCH_W // LANES)]
        return (jnp.concatenate(parts, axis=1) * _silu(gate)).astype(BF16)

    y_c = gated_group_norm(ocf_ref[...] + ocb_ref[...], cn_ref[...], cg_ref[...])
    y_d = gated_group_norm(odf_ref[...] + odb_ref[...], dn_ref[...], dg_ref[...])
    branches = (ya_ref[...], yb_ref[...], y_c, y_d)
    acc = None
    for j, yb in enumerate(branches):
        gate = jnp.dot(h, wm_ref[:, j * D_MODEL:(j + 1) * D_MODEL], preferred_element_type=F32)
        term = jax.nn.sigmoid(gate) * jnp.dot(yb, wb_ref[j], preferred_element_type=F32)
        acc = term if acc is None else acc + term
    y = jnp.dot(acc.astype(BF16), wo_ref[...], preferred_element_type=F32)
    o_ref[...] = x + m[2:3, :] * _rms(y, gain_ref[1:2, :])


def _post(xs, mod, gains, ya, yb, ocf, ocb, odf, odb, zc, zd, c_norm, d_norm, wm, wb, wo, bsz, n_rows):
    tm = TM_POST
    n_lat_tiles = bsz * SEQ // tm
    tiles_per_seq = SEQ // tm

    def mod_idx(i):
        return (jnp.where(i < n_lat_tiles, i // tiles_per_seq, bsz), 0, 0)

    row = lambda w, col=0: pl.BlockSpec((tm, w), lambda i: (i, col))
    const = lambda shape: pl.BlockSpec(shape, lambda i: (0,) * len(shape))
    return pl.pallas_call(
        _post_kernel,
        grid=(n_rows // tm,),
        in_specs=[row(D_MODEL), pl.BlockSpec((1, 8, D_MODEL), mod_idx), const((4, D_MODEL)),
                  row(BRANCH_W), row(BRANCH_W), row(BRANCH_W), row(BRANCH_W), row(BRANCH_W), row(BRANCH_W),
                  row(BRANCH_W, 4), row(BRANCH_W, 2),
                  const((1, LANES)), const((1, LANES)),
                  const((D_MODEL, N_BRANCH * D_MODEL)), const((N_BRANCH, BRANCH_W, D_MODEL)),
                  const((D_MODEL, D_MODEL))],
        out_specs=row(D_MODEL),
        out_shape=jax.ShapeDtypeStruct((n_rows, D_MODEL), F32),
        compiler_params=_cparams(("parallel",)),
        name="merge_out",
    )(xs, mod, gains, ya, yb, ocf, ocb, odf, odb, zc, zd, c_norm, d_norm, wm, wb, wo)


def _ffn_kernel(x_ref, mod_ref, gain_ref, w1_ref, w2_ref, o_ref, h_scr, acc_scr):
    k = pl.program_id(1)

    @pl.when(k == 0)
    def _():
        m = mod_ref[0]
        h_scr[...] = _norm_mod(x_ref[...], gain_ref[2:3, :], m[4:5, :], m[3:4, :]).astype(BF16)
        acc_scr[...] = jnp.zeros_like(acc_scr)

    u = jnp.dot(h_scr[...], w1_ref[...], preferred_element_type=F32)
    u = jnp.square(jnp.maximum(u, 0.0)).astype(BF16)
    acc_scr[...] += jnp.dot(u, w2_ref[...], preferred_element_type=F32)

    @pl.when(k == pl.num_programs(1) - 1)
    def _():
        m = mod_ref[0]
        o_ref[...] = x_ref[...] + m[5:6, :] * _rms(acc_scr[...], gain_ref[3:4, :])


def _ffn(xs, mod, gains, w1, w2, bsz, n_rows):
    n_lat_tiles = bsz * SEQ // TM
    tiles_per_seq = SEQ // TM

    def mod_idx(i, k):
        return (jnp.where(i < n_lat_tiles, i // tiles_per_seq, bsz), 0, 0)

    return pl.pallas_call(
        _ffn_kernel,
        grid=(n_rows // TM, D_FF // FF_CHUNK),
        in_specs=[pl.BlockSpec((TM, D_MODEL), lambda i, k: (i, 0)),
                  pl.BlockSpec((1, 8, D_MODEL), mod_idx),
                  pl.BlockSpec((4, D_MODEL), lambda i, k: (0, 0)),
                  pl.BlockSpec((D_MODEL, FF_CHUNK), lambda i, k: (0, k)),
                  pl.BlockSpec((FF_CHUNK, D_MODEL), lambda i, k: (k, 0))],
        out_specs=pl.BlockSpec((TM, D_MODEL), lambda i, k: (i, 0)),
        out_shape=jax.ShapeDtypeStruct((n_rows, D_MODEL), F32),
        scratch_shapes=[pltpu.VMEM((TM, D_MODEL), BF16), pltpu.VMEM((TM, D_MODEL), F32)],
        compiler_params=_cparams(("parallel", "arbitrary")),
        name="mlp",
    )(xs, mod, gains, w1, w2)


def _split_w_in(w_in_l):
    a_w = A_HEADS * HEAD_DIM + 2 * A_KV_HEADS * HEAD_DIM
    b_w = 3 * B_HEADS * HEAD_DIM
    c_w = 5 * C_HEADS * C_DK
    o_a, o_b, o_c = 0, a_w, a_w + b_w
    o_d = o_c + c_w
    kw = D_HEADS * D_DK
    d_q = w_in_l[:, o_d:o_d + kw]
    d_k = w_in_l[:, o_d + kw:o_d + 2 * kw]
    d_v = w_in_l[:, o_d + 2 * kw:o_d + 2 * kw + 512]
    o_gk = o_d + 2 * kw + 512
    d_gk = w_in_l[:, o_gk:o_gk + 2 * D_GATE_RANK]
    d_g = w_in_l[:, o_gk + 2 * D_GATE_RANK:o_gk + 2 * D_GATE_RANK + 512]
    o_m = o_gk + 2 * D_GATE_RANK + 512
    pad = jnp.zeros((D_MODEL, LANES - 2 * D_GATE_RANK), w_in_l.dtype)
    g_d = jnp.concatenate([d_q, d_k, d_v, d_g, d_gk, pad], axis=1)
    groups = (w_in_l[:, o_a:o_b], w_in_l[:, o_b:o_c], w_in_l[:, o_c:o_d], g_d, w_in_l[:, o_m:])
    return tuple(g.astype(BF16) for g in groups)


def kernel(x, c, ctx, c_ctx, w_mod, b_mod, norm_gains, w_in, a_sink, b_rel_bias, c_lower_bounds, c_norm,
           d_gate_up, d_gate_bias, d_norm, w_branch, w_out, w_ff1, w_ff2):
    bsz = x.shape[0]
    n_lat = bsz * SEQ
    n_rows = n_lat + bsz * CTX_LEN

    lb_soft = jax.nn.softmax(c_lower_bounds.astype(F32), axis=0)
    lb_all = jnp.cumsum(lb_soft, axis=0) - lb_soft[0:1]

    c_rows = jnp.concatenate([c, c_ctx[None, :], jnp.zeros((-(bsz + 1) % 8, D_MODEL), F32)], axis=0)
    mod_all = _modulation(c_rows, w_mod, b_mod)[:, :bsz + 1]
    mod_all = mod_all.reshape(DEPTH, bsz + 1, 6, D_MODEL)
    mod_all = jnp.pad(mod_all, ((0, 0), (0, 0), (0, 2), (0, 0)))

    rope = _rope_tables()
    bias_cls = _na_bias_classes(b_rel_bias)
    no_sink = jnp.full((B_HEADS,), NEG_INF, F32)
    xs = jnp.concatenate([x.reshape(n_lat, D_MODEL), ctx.reshape(bsz * CTX_LEN, D_MODEL)], axis=0)

    for l in range(DEPTH):
        need_ctx = l < DEPTH - 1
        mod, gains = mod_all[l], norm_gains[l]
        w_a, w_b, w_c, w_d, w_m = _split_w_in(w_in[l])
        za = _in_proj(xs, mod, gains, w_a, w_a.shape[1], BF16, bsz, rope=rope)
        zb = _in_proj(xs, mod, gains, w_b, w_b.shape[1], BF16, bsz)
        zc = _in_proj(xs, mod, gains, w_c, w_c.shape[1], F32, bsz)
        zd = _in_proj(xs, mod, gains, w_d, GD_WIDTH, F32, bsz)

        ya = _attn_a(za, a_sink[l], bsz)
        yb = _attn_b(zb, bias_cls[l], bsz)
        if need_ctx:
            ya = jnp.concatenate([ya, _attn_ctx(za, a_sink[l], bsz, A_HEADS, A_KV_HEADS)], axis=0)
            yb = jnp.concatenate([yb, _attn_ctx(zb, no_sink, bsz, B_HEADS, B_HEADS)], axis=0)
        ocf, ocb, odf, odb = _scan(zc, zd, lb_all[l].reshape(1, -1), d_gate_up[l],
                                   d_gate_bias[l].reshape(2, 1, -1), bsz)

        rows_l = n_rows if need_ctx else n_lat
        x_mid = _post(xs, mod, gains, ya, yb, ocf, ocb, odf, odb, zc, zd,
                      c_norm[l].reshape(1, -1), d_norm[l].reshape(1, -1),
                      w_m, w_branch[l].astype(BF16), w_out[l].astype(BF16), bsz, rows_l)
        xs = _ffn(x_mid, mod, gains, w_ff1[l].astype(BF16), w_ff2[l].astype(BF16), bsz, rows_l)
    return xs[:n_lat].reshape(bsz, SEQ, D_MODEL)
```

```python
import functools

import numpy as np
import jax
import jax.numpy as jnp
from jax import lax
from jax.experimental import pallas as pl
from jax.experimental.pallas import tpu as pltpu

D_MODEL = 1024
SEQ = 2048
DEPTH = 4
CTX_LEN = 256
GRID_W = 64
HEAD_DIM = 64
A_HEADS = 8
A_KV_HEADS = 2
A_WINDOW = 128
A_BLOCK = 128
B_HEADS = 8
NA_KH = 8
NA_KW = 16
C_HEADS = 4
C_DK = 128
D_HEADS = 4
D_DK = 64
D_GATE_RANK = 16
D_GATE_NORM = 16.0
N_BRANCH = 4
D_FF = 4 * D_MODEL
ROPE_BASE = 10000.0
EPS = 1e-6
NEG_INF = -1e30
BRANCH_W = 512

F32 = jnp.float32
BF16 = jnp.bfloat16

LANES = 128
TM = 512
TM_POST = 512
SUB_ROWS = 256
FF_CHUNK = 1024
SCAN_T = 64
VMEM_LIMIT = 52 * 1024 * 1024

GD_WIDTH = 1664

_NT = (((1,), (1,)), ((), ()))
_TN = (((0,), (0,)), ((), ()))


def _cparams(sem):
    return pltpu.CompilerParams(dimension_semantics=sem, vmem_limit_bytes=VMEM_LIMIT)


def _const_spec(shape):
    return pl.BlockSpec(shape, lambda *_: (0,) * len(shape), pipeline_mode=pl.Buffered(1))


def _rms(x, gain):
    return x * lax.rsqrt(jnp.mean(x * x, axis=-1, keepdims=True) + EPS) * gain


def _norm_mod(x, gain, sc, sh):
    return _rms(x, gain) * (1.0 + sc) + sh


def _silu(x):
    return x * jax.nn.sigmoid(x)


def _log_sigmoid(x):
    return jnp.minimum(x, 0.0) - jnp.log1p(jnp.exp(-jnp.abs(x)))


def _mod_kernel(c_ref, w_ref, b_ref, o_ref):
    a = _silu(c_ref[...]).astype(BF16)
    o_ref[0] = jnp.dot(a, w_ref[0].astype(BF16), preferred_element_type=F32) + b_ref[0]


def _modulation(c_rows, w_mod, b_mod):
    rows = c_rows.shape[0]
    return pl.pallas_call(
        _mod_kernel,
        grid=(DEPTH, 6),
        in_specs=[pl.BlockSpec((rows, D_MODEL), lambda l, j: (0, 0)),
                  pl.BlockSpec((1, D_MODEL, D_MODEL), lambda l, j: (l, 0, j)),
                  pl.BlockSpec((1, 1, D_MODEL), lambda l, j: (l, 0, j))],
        out_specs=pl.BlockSpec((1, rows, D_MODEL), lambda l, j: (l, 0, j)),
        out_shape=jax.ShapeDtypeStruct((DEPTH, rows, 6 * D_MODEL), F32),
        compiler_params=_cparams(("parallel", "parallel")),
        name="modulation",
    )(c_rows, w_mod, b_mod.reshape(DEPTH, 1, 6 * D_MODEL))


ROPE_SLABS = 5


def _in_proj_kernel(x_ref, mod_ref, gain_ref, wa_ref, wb_ref, wc_ref, wd_ref, cos_ref, s1_ref, s2_ref,
                    za_ref, zb_ref, zc_ref, zd_ref):
    m = mod_ref[0]
    for r in range(TM // SUB_ROWS):
        rs = slice(r * SUB_ROWS, (r + 1) * SUB_ROWS)
        h = _norm_mod(x_ref[rs, :], gain_ref[0:1, :], m[1:2, :], m[0:1, :]).astype(BF16)
        z = jnp.dot(h, wa_ref[...], preferred_element_type=F32)
        cos, s1, s2 = cos_ref[rs, :], s1_ref[rs, :], s2_ref[rs, :]
        for s in range(ROPE_SLABS):
            zs = z[:, s * LANES:(s + 1) * LANES]
            rot = (zs * cos + pltpu.roll(zs, LANES - 16, axis=1) * s1
                   + pltpu.roll(zs, 16, axis=1) * s2)
            za_ref[rs, s * LANES:(s + 1) * LANES] = rot.astype(za_ref.dtype)
        za_ref[rs, ROPE_SLABS * LANES:] = z[:, ROPE_SLABS * LANES:].astype(za_ref.dtype)
        for w_ref, z_ref in ((wb_ref, zb_ref), (wc_ref, zc_ref), (wd_ref, zd_ref)):
            z_ref[rs, :] = jnp.dot(h, w_ref[...], preferred_element_type=F32).astype(z_ref.dtype)


def _in_proj(xs, mod, gains, weights, rope, bsz):
    rows = xs.shape[0]
    n_lat_tiles = bsz * SEQ // TM
    tiles_per_seq = SEQ // TM

    def mod_idx(i):
        return (jnp.where(i < n_lat_tiles, i // tiles_per_seq, bsz), 0, 0)

    def rope_idx(i):
        return (jnp.where(i < n_lat_tiles, i % tiles_per_seq, tiles_per_seq), 0)

    out_dtypes = (BF16, BF16, F32, F32)
    return pl.pallas_call(
        _in_proj_kernel,
        grid=(rows // TM,),
        in_specs=([pl.BlockSpec((TM, D_MODEL), lambda i: (i, 0)),
                   pl.BlockSpec((1, 8, D_MODEL), mod_idx),
                   _const_spec((4, D_MODEL))]
                  + [_const_spec(w.shape) for w in weights]
                  + [pl.BlockSpec((TM, LANES), rope_idx)] * 3),
        out_specs=[pl.BlockSpec((TM, w.shape[1]), lambda i: (i, 0)) for w in weights],
        out_shape=[jax.ShapeDtypeStruct((rows, w.shape[1]), dt) for w, dt in zip(weights, out_dtypes)],
        compiler_params=_cparams(("parallel",)),
        name="in_proj",
    )(xs, mod, gains, *weights, *rope)


def _rope_tables():
    t = jnp.arange(SEQ)
    row = (t // GRID_W).astype(F32)
    col = (t % GRID_W).astype(F32)
    half = HEAD_DIM // 2
    inv = ROPE_BASE ** (-jnp.arange(0, half, 2, dtype=F32) / half)
    lane = np.arange(LANES)
    hl = lane % HEAD_DIM
    use_row = (hl // half) == 0
    freq = hl % (half // 2)
    first = (hl % half) < (half // 2)
    ang = jnp.where(use_row[None, :], row[:, None], col[:, None]) * inv[freq][None, :]
    cos, sin = jnp.cos(ang), jnp.sin(ang)
    s1 = jnp.where(first[None, :], -sin, 0.0)
    s2 = jnp.where(first[None, :], 0.0, sin)
    ident = jnp.ones((TM, LANES), F32)
    zero = jnp.zeros((TM, LANES), F32)
    return (jnp.concatenate([cos, ident]), jnp.concatenate([s1, zero]), jnp.concatenate([s2, zero]))


def _attn_a_kernel(sink_ref, q_ref, kp_ref, kc_ref, kn_ref, vp_ref, vc_ref, vn_ref, kx_ref, vx_ref, mask_ref,
                   o_ref):
    grp = A_HEADS // A_KV_HEADS
    rows = grp * A_BLOCK
    scale = HEAD_DIM ** -0.5
    mask = jnp.concatenate([mask_ref[0]] * grp, axis=0)
    head_of_row = lax.broadcasted_iota(jnp.int32, (rows, 1), 0) >> 7

    def scores(g):
        hs = slice(g * HEAD_DIM, (g + 1) * HEAD_DIM)
        q = jnp.concatenate([q_ref[:, (g * grp + j) * HEAD_DIM:(g * grp + j + 1) * HEAD_DIM]
                             for j in range(grp)], axis=0) * scale
        k = jnp.concatenate([kp_ref[:, hs], kc_ref[:, hs], kn_ref[:, hs], kx_ref[:, hs]], axis=0)
        return lax.dot_general(q, k, _NT, preferred_element_type=F32)

    all_scores = [scores(g) for g in range(A_KV_HEADS)]
    for g in range(A_KV_HEADS):
        hs = slice(g * HEAD_DIM, (g + 1) * HEAD_DIM)
        v = jnp.concatenate([vp_ref[:, hs], vc_ref[:, hs], vn_ref[:, hs], vx_ref[:, hs]], axis=0)
        s = all_scores[g] + mask
        sink = jnp.zeros((rows, 1), F32)
        for j in range(grp):
            sink = jnp.where(head_of_row == j, sink_ref[g * grp + j], sink)
        m = jnp.maximum(jnp.max(s, axis=-1, keepdims=True), sink)
        p = jnp.exp(s - m)
        den = jnp.sum(p, axis=-1, keepdims=True) + jnp.exp(sink - m)
        o = jnp.dot(p.astype(BF16), v, preferred_element_type=F32) / den
        o = jnp.concatenate([o[j * A_BLOCK:(j + 1) * A_BLOCK] for j in range(grp)], axis=1)
        o_ref[:, g * grp * HEAD_DIM:(g + 1) * grp * HEAD_DIM] = o.astype(o_ref.dtype)


def _window_masks():
    nb = SEQ // A_BLOCK
    qi = np.arange(A_BLOCK)[:, None]
    kj = np.arange(3 * A_BLOCK)[None, :]
    out = []
    for i in (0, 1, nb - 1):
        qpos = i * A_BLOCK + qi
        kpos = (i - 1) * A_BLOCK + kj
        ok = (np.abs(kpos - qpos) <= A_WINDOW) & (kpos >= 0) & (kpos < SEQ)
        ok = np.concatenate([ok, np.ones((A_BLOCK, CTX_LEN), bool)], axis=1)
        out.append(np.where(ok, 0.0, NEG_INF).astype(np.float32))
    return jnp.asarray(np.stack(out))


def _attn_a(za, sink, bsz):
    nb = SEQ // A_BLOCK
    qcb = A_HEADS * HEAD_DIM // LANES
    ctx_blk0 = bsz * SEQ // CTX_LEN
    nkeys = 3 * A_BLOCK + CTX_LEN

    def mask_idx(b, i):
        return (jnp.where(i == 0, 0, jnp.where(i == nb - 1, 2, 1)), 0, 0)

    def kv_spec(col, off):
        return pl.BlockSpec((A_BLOCK, LANES),
                            lambda b, i: (b * nb + jnp.clip(i + off, 0, nb - 1), col))

    return pl.pallas_call(
        _attn_a_kernel,
        grid=(bsz, nb),
        in_specs=[pl.BlockSpec(memory_space=pltpu.SMEM),
                  pl.BlockSpec((A_BLOCK, A_HEADS * HEAD_DIM), lambda b, i: (b * nb + i, 0)),
                  kv_spec(qcb, -1), kv_spec(qcb, 0), kv_spec(qcb, 1),
                  kv_spec(qcb + 1, -1), kv_spec(qcb + 1, 0), kv_spec(qcb + 1, 1),
                  pl.BlockSpec((CTX_LEN, LANES), lambda b, i: (ctx_blk0 + b, qcb)),
                  pl.BlockSpec((CTX_LEN, LANES), lambda b, i: (ctx_blk0 + b, qcb + 1)),
                  pl.BlockSpec((1, A_BLOCK, nkeys), mask_idx)],
        out_specs=pl.BlockSpec((A_BLOCK, BRANCH_W), lambda b, i: (b * nb + i, 0)),
        out_shape=jax.ShapeDtypeStruct((bsz * SEQ, BRANCH_W), BF16),
        compiler_params=_cparams(("parallel", "parallel")),
        name="attn_window",
    )(sink, za, za, za, za, za, za, za, za, za, _window_masks())


NA_QROWS = 4
NA_KROWS = 12
NA_QTOK = NA_QROWS * GRID_W
NA_KEYS = NA_KROWS * GRID_W + CTX_LEN


def _attn_b_kernel(q_ref, k0_ref, k1_ref, k2_ref, v0_ref, v1_ref, v2_ref, kx_ref, vx_ref, bias_ref, o_ref):
    scale = HEAD_DIM ** -0.5

    def scores(h):
        hs = slice(h * HEAD_DIM, (h + 1) * HEAD_DIM)
        q = q_ref[:, hs] * scale
        k = jnp.concatenate([k0_ref[:, hs], k1_ref[:, hs], k2_ref[:, hs], kx_ref[:, hs]], axis=0)
        return lax.dot_general(q, k, _NT, preferred_element_type=F32)

    outs = []
    s_next = scores(0)
    for h in range(B_HEADS):
        hs = slice(h * HEAD_DIM, (h + 1) * HEAD_DIM)
        s = s_next + bias_ref[0, h]
        if h + 1 < B_HEADS:
            s_next = scores(h + 1)
        v = jnp.concatenate([v0_ref[:, hs], v1_ref[:, hs], v2_ref[:, hs], vx_ref[:, hs]], axis=0)
        m = jnp.max(s, axis=-1, keepdims=True)
        p = jnp.exp(s - m)
        den = jnp.sum(p, axis=-1, keepdims=True)
        outs.append(jnp.dot(p.astype(BF16), v, preferred_element_type=F32) / den)
    o_ref[...] = jnp.concatenate(outs, axis=1).astype(o_ref.dtype)


def _bias_expand_kernel(rb_ref, onehot_ref, o_ref):
    o_ref[...] = jnp.dot(rb_ref[...], onehot_ref[...], preferred_element_type=F32,
                         precision=lax.Precision.HIGHEST)


def _na_bias_classes(rel_bias_all):
    rows = SEQ // GRID_W
    n_dr, n_dc = 2 * NA_KH - 1, 2 * NA_KW - 1
    qc = np.arange(GRID_W)[:, None]
    kc = np.arange(GRID_W)[None, :]
    cstart = np.clip(qc - NA_KW // 2, 0, GRID_W - NA_KW)
    col_ok = (kc >= cstart) & (kc < cstart + NA_KW)
    dc_idx = np.clip(kc - qc, -(NA_KW - 1), NA_KW - 1) + (NA_KW - 1)
    onehot = (np.arange(n_dc + 1)[:, None] == dc_idx.reshape(1, -1)).astype(np.float32)
    n_tab = DEPTH * B_HEADS * n_dr
    rb = jnp.pad(rel_bias_all.astype(F32).reshape(n_tab, n_dc), ((0, 0), (0, 1)))
    dense = pl.pallas_call(
        _bias_expand_kernel,
        grid=(DEPTH,),
        in_specs=[pl.BlockSpec((n_tab // DEPTH, n_dc + 1), lambda l: (l, 0)),
                  pl.BlockSpec((n_dc + 1, GRID_W * GRID_W), lambda l: (0, 0))],
        out_specs=pl.BlockSpec((n_tab // DEPTH, GRID_W * GRID_W), lambda l: (l, 0)),
        out_shape=jax.ShapeDtypeStruct((n_tab, GRID_W * GRID_W), F32),
        compiler_params=_cparams(("parallel",)),
        name="bias_expand",
    )(rb, jnp.asarray(onehot))
    dense = dense.reshape(DEPTH, B_HEADS, n_dr, GRID_W, GRID_W)
    masked = jnp.where(col_ok[None, None, None], dense, NEG_INF)
    n_groups = rows // NA_QROWS
    out = []
    for g in (0, 1, n_groups - 1):
        base = _na_key_base(g)
        per_row = []
        for lr in range(NA_QROWS):
            r = g * NA_QROWS + lr
            r0 = min(max(r - NA_KH // 2, 0), rows - NA_KH)
            dr_lo = r0 - r + (NA_KH - 1)
            before = r0 - base
            piece = jnp.pad(masked[:, :, dr_lo:dr_lo + NA_KH],
                            ((0, 0), (0, 0), (before, NA_KROWS - NA_KH - before), (0, 0), (0, 0)),
                            constant_values=NEG_INF)
            per_row.append(piece.transpose(0, 1, 3, 2, 4).reshape(DEPTH, B_HEADS, GRID_W, NA_KROWS * GRID_W))
        cls = jnp.concatenate(per_row, axis=2)
        out.append(jnp.pad(cls, ((0, 0), (0, 0), (0, 0), (0, CTX_LEN))))
    return jnp.stack(out, axis=1)


def _na_key_base(g):
    rows = SEQ // GRID_W
    return min(max(g * NA_QROWS - NA_KH // 2, 0), rows - NA_KROWS)


def _attn_b(zb, bias_cls, bsz):
    n_groups = SEQ // NA_QTOK
    ctx_blk0 = bsz * SEQ // CTX_LEN
    max_base_blk = _na_key_base(n_groups - 1) // NA_QROWS

    def cls_idx(g, b):
        return (jnp.where(g == 0, 0, jnp.where(g == n_groups - 1, 2, 1)), 0, 0, 0)

    def kv_spec(col, j):
        return pl.BlockSpec((NA_QTOK, BRANCH_W),
                            lambda g, b: (b * n_groups + jnp.clip(g - 1, 0, max_base_blk) + j, col))

    return pl.pallas_call(
        _attn_b_kernel,
        grid=(n_groups, bsz),
        in_specs=[pl.BlockSpec((NA_QTOK, BRANCH_W), lambda g, b: (b * n_groups + g, 0)),
                  kv_spec(1, 0), kv_spec(1, 1), kv_spec(1, 2),
                  kv_spec(2, 0), kv_spec(2, 1), kv_spec(2, 2),
                  pl.BlockSpec((CTX_LEN, BRANCH_W), lambda g, b: (ctx_blk0 + b, 1)),
                  pl.BlockSpec((CTX_LEN, BRANCH_W), lambda g, b: (ctx_blk0 + b, 2)),
                  pl.BlockSpec((1, B_HEADS, NA_QTOK, NA_KEYS), cls_idx)],
        out_specs=pl.BlockSpec((NA_QTOK, BRANCH_W), lambda g, b: (b * n_groups + g, 0)),
        out_shape=jax.ShapeDtypeStruct((bsz * SEQ, BRANCH_W), BF16),
        compiler_params=_cparams(("arbitrary", "arbitrary")),
        name="attn_neighbourhood",
    )(zb, zb, zb, zb, zb, zb, zb, zb, zb, bias_cls)


def _attn_ctx_kernel(sink_ref, q_ref, k_ref, v_ref, o_ref, *, n_heads, n_kv):
    grp = n_heads // n_kv
    rows = grp * CTX_LEN
    scale = HEAD_DIM ** -0.5
    head_of_row = lax.broadcasted_iota(jnp.int32, (rows, 1), 0) >> 8
    for g in range(n_kv):
        hs = slice(g * HEAD_DIM, (g + 1) * HEAD_DIM)
        q = jnp.concatenate([q_ref[:, (g * grp + j) * HEAD_DIM:(g * grp + j + 1) * HEAD_DIM]
                             for j in range(grp)], axis=0)
        s = lax.dot_general(q, k_ref[:, hs], _NT, preferred_element_type=F32) * scale
        sink = jnp.zeros((rows, 1), F32)
        for j in range(grp):
            sink = jnp.where(head_of_row == j, sink_ref[g * grp + j], sink)
        m = jnp.maximum(jnp.max(s, axis=-1, keepdims=True), sink)
        p = jnp.exp(s - m)
        den = jnp.sum(p, axis=-1, keepdims=True) + jnp.exp(sink - m)
        o = jnp.dot(p.astype(BF16), v_ref[:, hs], preferred_element_type=F32) / den
        o = jnp.concatenate([o[j * CTX_LEN:(j + 1) * CTX_LEN] for j in range(grp)], axis=1)
        o_ref[:, g * grp * HEAD_DIM:(g + 1) * grp * HEAD_DIM] = o.astype(o_ref.dtype)


def _attn_ctx(z, sink, bsz, n_heads, n_kv):
    ctx_blk0 = bsz * SEQ // CTX_LEN
    kvw = n_kv * HEAD_DIM
    kcol = n_heads * HEAD_DIM // kvw
    return pl.pallas_call(
        functools.partial(_attn_ctx_kernel, n_heads=n_heads, n_kv=n_kv),
        grid=(bsz,),
        in_specs=[pl.BlockSpec(memory_space=pltpu.SMEM),
                  pl.BlockSpec((CTX_LEN, n_heads * HEAD_DIM), lambda b: (ctx_blk0 + b, 0)),
                  pl.BlockSpec((CTX_LEN, kvw), lambda b: (ctx_blk0 + b, kcol)),
                  pl.BlockSpec((CTX_LEN, kvw), lambda b: (ctx_blk0 + b, kcol + 1))],
        out_specs=pl.BlockSpec((CTX_LEN, BRANCH_W), lambda b: (b, 0)),
        out_shape=jax.ShapeDtypeStruct((bsz * CTX_LEN, BRANCH_W), BF16),
        compiler_params=_cparams(("parallel",)),
        name="attn_ctx_%d" % n_kv,
    )(sink, z, z, z)


def _scan_chunks(chains, tris):
    tribs = [tri.astype(BF16) for tri in tris]
    cums = []
    for (q, k, v, g, state, d, n_heads, dk) in chains:
        g_hi = g.astype(BF16)
        rem = g - g_hi.astype(F32)
        g_mid = rem.astype(BF16)
        g_lo = (rem - g_mid.astype(F32)).astype(BF16)
        cums.append(jnp.dot(tribs[d], g_hi, preferred_element_type=F32)
                    + jnp.dot(tribs[d], g_mid, preferred_element_type=F32)
                    + jnp.dot(tribs[d], g_lo, preferred_element_type=F32))
    prepped = []
    for (q, k, v, g, state, d, n_heads, dk), cum in zip(chains, cums):
        t = q.shape[0]
        tot = cum[t - 1:t] if d == 0 else cum[0:1]
        mid = 0.5 * tot
        e_mid = jnp.exp(mid)
        qa = q * jnp.exp(cum - mid)
        kb = k * jnp.exp(mid - cum)
        qd = (qa * e_mid).astype(BF16)
        kd = (kb * e_mid).astype(BF16)
        prepped.append((qa.astype(BF16), kb.astype(BF16), qd, kd, v.astype(BF16),
                        state * jnp.exp(tot), state.astype(BF16)))
    partial = []
    for (q, k, v, g, state, d, n_heads, dk), (qa, kb, qd, kd, vb, decayed, state_b) in zip(chains, prepped):
        per_head = []
        for h in range(n_heads):
            ks = slice(h * dk, (h + 1) * dk)
            vs = slice(h * LANES, (h + 1) * LANES)
            a = lax.dot_general(qa[:, ks], kb[:, ks], _NT, preferred_element_type=F32)
            carry = lax.dot_general(qd[:, ks], state_b[:, ks], _NT, preferred_element_type=F32)
            upd = lax.dot_general(vb[:, vs], kd[:, ks], _TN, preferred_element_type=F32)
            per_head.append((a, carry, decayed[:, ks] + upd))
        partial.append(per_head)
    results = []
    for (q, k, v, g, state, d, n_heads, dk), (qa, kb, qd, kd, vb, decayed, state_b), per_head in zip(
            chains, prepped, partial):
        outs = []
        for h, (a, carry, new_s) in enumerate(per_head):
            a = jnp.where(tris[d], a, 0.0).astype(BF16)
            outs.append(jnp.dot(a, vb[:, h * LANES:(h + 1) * LANES], preferred_element_type=F32) + carry)
        results.append((jnp.concatenate(outs, axis=1), jnp.concatenate([p[2] for p in per_head], axis=1)))
    return results


def _scan_kernel(cq_f, cf_f, ci_f, cq_b, cf_b, ci_b, dqk_f, dv_f, dgk_f, dqk_b, dv_b, dgk_b,
                 lb_ref, up_ref, gb_ref, ocf_ref, ocb_ref, odf_ref, odb_ref, scf_scr, scb_scr, sdf_scr, sdb_scr):
    scratches = (scf_scr, scb_scr, sdf_scr, sdb_scr)

    @pl.when(pl.program_id(1) == 0)
    def _():
        for s in scratches:
            s[...] = jnp.zeros_like(s)

    states = [s[...] for s in scratches]
    t = SCAN_T
    row = lax.broadcasted_iota(jnp.int32, (t, t), 0)
    col = lax.broadcasted_iota(jnp.int32, (t, t), 1)
    lb = lb_ref[...]
    kw = D_HEADS * D_DK
    dirs = ((cq_f, cf_f, ci_f, dqk_f, dv_f, dgk_f), (cq_b, cf_b, ci_b, dqk_b, dv_b, dgk_b))
    tris = (col <= row, col >= row)
    graws = [jnp.dot(dgk[...][:, d * D_GATE_RANK:(d + 1) * D_GATE_RANK], up_ref[d], preferred_element_type=F32,
                     precision=lax.Precision.HIGHEST) + gb_ref[d]
             for d, (cq, cf, ci, dqk, dv, dgk) in enumerate(dirs)]
    chains = []
    for d, (cq, cf, ci, dqk, dv, dgk) in enumerate(dirs):
        forget = jax.nn.sigmoid(cf[...])
        g = jnp.log(lb + (1.0 - lb) * forget)
        key = (1.0 - lb) * (1.0 - forget)
        chains.append((_silu(cq[...]), key, ci[...], g, states[d], d, C_HEADS, C_DK))
        g = _log_sigmoid(graws[d]) / D_GATE_NORM
        qk = dqk[...]
        chains.append((qk[:, :kw] * (D_DK ** -0.5), qk[:, kw:], dv[...], g, states[2 + d], d, D_HEADS, D_DK))
    (ocf, scf), (odf, sdf), (ocb, scb), (odb, sdb) = _scan_chunks(chains, tris)
    for ref, val in ((ocf_ref, ocf), (ocb_ref, ocb), (odf_ref, odf), (odb_ref, odb),
                     (scf_scr, scf), (scb_scr, scb), (sdf_scr, sdf), (sdb_scr, sdb)):
        ref[...] = val


def _scan(zc, zd, lb, gate_up, gate_bias, bsz):
    t = SCAN_T
    n_ctx = CTX_LEN // t
    n_lat = SEQ // t
    ctx_blk0 = bsz * n_lat

    def fwd(b, c):
        return jnp.where(c < n_ctx, ctx_blk0 + b * n_ctx + c, b * n_lat + c - n_ctx)

    def bwd(b, c):
        return jnp.where(c < n_ctx, ctx_blk0 + b * n_ctx + (n_ctx - 1 - c), b * n_lat + (n_lat + n_ctx - 1 - c))

    def spec(order, width, col):
        return pl.BlockSpec((t, width), lambda b, c: (order(b, c), col))

    gk_col = (GD_WIDTH - LANES) // LANES
    in_specs = [spec(fwd, 512, 0), spec(fwd, 512, 1), spec(fwd, 512, 3),
                spec(bwd, 512, 0), spec(bwd, 512, 2), spec(bwd, 512, 3),
                spec(fwd, 512, 0), spec(fwd, 512, 1), spec(fwd, LANES, gk_col),
                spec(bwd, 512, 0), spec(bwd, 512, 1), spec(bwd, LANES, gk_col),
                pl.BlockSpec((1, 512), lambda b, c: (0, 0)),
                pl.BlockSpec((2, D_GATE_RANK, D_HEADS * D_DK), lambda b, c: (0, 0, 0)),
                pl.BlockSpec((2, 1, D_HEADS * D_DK), lambda b, c: (0, 0, 0))]
    rows = zc.shape[0]
    out = jax.ShapeDtypeStruct((rows, BRANCH_W), F32)
    return pl.pallas_call(
        _scan_kernel,
        grid=(bsz, n_ctx + n_lat),
        in_specs=in_specs,
        out_specs=[spec(fwd, 512, 0), spec(bwd, 512, 0), spec(fwd, 512, 0), spec(bwd, 512, 0)],
        out_shape=[out, out, out, out],
        scratch_shapes=[pltpu.VMEM((LANES, C_HEADS * C_DK), F32), pltpu.VMEM((LANES, C_HEADS * C_DK), F32),
                        pltpu.VMEM((LANES, D_HEADS * D_DK), F32), pltpu.VMEM((LANES, D_HEADS * D_DK), F32)],
        compiler_params=_cparams(("parallel", "arbitrary")),
        name="bidir_scan",
    )(zc, zc, zc, zc, zc, zc, zd, zd, zd, zd, zd, zd, lb, gate_up, gate_bias)


def _post_kernel(x_ref, mod_ref, gain_ref, ya_ref, yb_ref, ocf_ref, ocb_ref, odf_ref, odb_ref,
                 cg_ref, dg_ref, cn_ref, dn_ref, wm_ref, wb_ref, wo_ref, o_ref):
    m = mod_ref[0]

    def gated_group_norm(o, gain, gate):
        parts = [_rms(o[:, j * LANES:(j + 1) * LANES], gain) for j in range(BRANCH_W // LANES)]
        return (jnp.concatenate(parts, axis=1) * _silu(gate)).astype(BF16)

    for r in range(TM_POST // SUB_ROWS):
        rs = slice(r * SUB_ROWS, (r + 1) * SUB_ROWS)
        x = x_ref[rs, :]
        h = _norm_mod(x, gain_ref[0:1, :], m[1:2, :], m[0:1, :]).astype(BF16)
        y_c = gated_group_norm(ocf_ref[rs, :] + ocb_ref[rs, :], cn_ref[...], cg_ref[rs, :])
        y_d = gated_group_norm(odf_ref[rs, :] + odb_ref[rs, :], dn_ref[...], dg_ref[rs, :])
        branches = (ya_ref[rs, :], yb_ref[rs, :], y_c, y_d)
        acc = None
        for j, yb in enumerate(branches):
            gate = jnp.dot(h, wm_ref[:, j * D_MODEL:(j + 1) * D_MODEL], preferred_element_type=F32)
            term = jax.nn.sigmoid(gate) * jnp.dot(yb, wb_ref[j], preferred_element_type=F32)
            acc = term if acc is None else acc + term
        y = jnp.dot(acc.astype(BF16), wo_ref[...], preferred_element_type=F32)
        o_ref[rs, :] = x + m[2:3, :] * _rms(y, gain_ref[1:2, :])


def _post(xs, mod, gains, ya, yb, ocf, ocb, odf, odb, zc, zd, c_norm, d_norm, wm, wb, wo, bsz, n_rows):
    tm = TM_POST
    n_lat_tiles = bsz * SEQ // tm
    tiles_per_seq = SEQ // tm

    def mod_idx(i):
        return (jnp.where(i < n_lat_tiles, i // tiles_per_seq, bsz), 0, 0)

    row = lambda w, col=0: pl.BlockSpec((tm, w), lambda i: (i, col))
    const = _const_spec
    return pl.pallas_call(
        _post_kernel,
        grid=(n_rows // tm,),
        in_specs=[row(D_MODEL), pl.BlockSpec((1, 8, D_MODEL), mod_idx), const((4, D_MODEL)),
                  row(BRANCH_W), row(BRANCH_W), row(BRANCH_W), row(BRANCH_W), row(BRANCH_W), row(BRANCH_W),
                  row(BRANCH_W, 4), row(BRANCH_W, 2),
                  const((1, LANES)), const((1, LANES)),
                  const((D_MODEL, N_BRANCH * D_MODEL)), const((N_BRANCH, BRANCH_W, D_MODEL)),
                  const((D_MODEL, D_MODEL))],
        out_specs=row(D_MODEL),
        out_shape=jax.ShapeDtypeStruct((n_rows, D_MODEL), F32),
        compiler_params=_cparams(("parallel",)),
        name="merge_out",
    )(xs, mod, gains, ya, yb, ocf, ocb, odf, odb, zc, zd, c_norm, d_norm, wm, wb, wo)


def _ffn_kernel(x_ref, mod_ref, gain_ref, w1_ref, w2_ref, o_ref):
    m = mod_ref[0]
    for r in range(TM // SUB_ROWS):
        rs = slice(r * SUB_ROWS, (r + 1) * SUB_ROWS)
        x = x_ref[rs, :]
        h = _norm_mod(x, gain_ref[2:3, :], m[4:5, :], m[3:4, :]).astype(BF16)
        acc = None
        for k in range(D_FF // FF_CHUNK):
            ks = slice(k * FF_CHUNK, (k + 1) * FF_CHUNK)
            u = jnp.dot(h, w1_ref[:, ks], preferred_element_type=F32)
            u = jnp.square(jnp.maximum(u, 0.0)).astype(BF16)
            part = jnp.dot(u, w2_ref[ks, :], preferred_element_type=F32)
            acc = part if acc is None else acc + part
        o_ref[rs, :] = x + m[5:6, :] * _rms(acc, gain_ref[3:4, :])


def _ffn(xs, mod, gains, w1, w2, bsz, n_rows):
    n_lat_tiles = bsz * SEQ // TM
    tiles_per_seq = SEQ // TM

    def mod_idx(i):
        return (jnp.where(i < n_lat_tiles, i // tiles_per_seq, bsz), 0, 0)

    return pl.pallas_call(
        _ffn_kernel,
        grid=(n_rows // TM,),
        in_specs=[pl.BlockSpec((TM, D_MODEL), lambda i: (i, 0)),
                  pl.BlockSpec((1, 8, D_MODEL), mod_idx),
                  _const_spec((4, D_MODEL)),
                  _const_spec((D_MODEL, D_FF)),
                  _const_spec((D_FF, D_MODEL))],
        out_specs=pl.BlockSpec((TM, D_MODEL), lambda i: (i, 0)),
        out_shape=jax.ShapeDtypeStruct((n_rows, D_MODEL), F32),
        compiler_params=_cparams(("parallel",)),
        name="mlp",
    )(xs, mod, gains, w1, w2)


def _split_w_in(w_in_l):
    a_w = A_HEADS * HEAD_DIM + 2 * A_KV_HEADS * HEAD_DIM
    b_w = 3 * B_HEADS * HEAD_DIM
    c_w = 5 * C_HEADS * C_DK
    o_a, o_b, o_c = 0, a_w, a_w + b_w
    o_d = o_c + c_w
    kw = D_HEADS * D_DK
    d_q = w_in_l[:, o_d:o_d + kw]
    d_k = w_in_l[:, o_d + kw:o_d + 2 * kw]
    d_v = w_in_l[:, o_d + 2 * kw:o_d + 2 * kw + 512]
    o_gk = o_d + 2 * kw + 512
    d_gk = w_in_l[:, o_gk:o_gk + 2 * D_GATE_RANK]
    d_g = w_in_l[:, o_gk + 2 * D_GATE_RANK:o_gk + 2 * D_GATE_RANK + 512]
    o_m = o_gk + 2 * D_GATE_RANK + 512
    pad = jnp.zeros((D_MODEL, LANES - 2 * D_GATE_RANK), w_in_l.dtype)
    g_d = jnp.concatenate([d_q, d_k, d_v, d_g, d_gk, pad], axis=1)
    groups = (w_in_l[:, o_a:o_b], w_in_l[:, o_b:o_c], w_in_l[:, o_c:o_d], g_d, w_in_l[:, o_m:])
    return tuple(g.astype(BF16) for g in groups)


def kernel(x, c, ctx, c_ctx, w_mod, b_mod, norm_gains, w_in, a_sink, b_rel_bias, c_lower_bounds, c_norm,
           d_gate_up, d_gate_bias, d_norm, w_branch, w_out, w_ff1, w_ff2):
    bsz = x.shape[0]
    n_lat = bsz * SEQ
    n_rows = n_lat + bsz * CTX_LEN

    lb_soft = jax.nn.softmax(c_lower_bounds.astype(F32), axis=0)
    lb_all = jnp.cumsum(lb_soft, axis=0) - lb_soft[0:1]

    c_rows = jnp.concatenate([c, c_ctx[None, :], jnp.zeros((-(bsz + 1) % 8, D_MODEL), F32)], axis=0)
    mod_all = _modulation(c_rows, w_mod, b_mod)[:, :bsz + 1]
    mod_all = mod_all.reshape(DEPTH, bsz + 1, 6, D_MODEL)
    mod_all = jnp.pad(mod_all, ((0, 0), (0, 0), (0, 2), (0, 0)))

    rope = _rope_tables()
    bias_cls = _na_bias_classes(b_rel_bias)
    no_sink = jnp.full((B_HEADS,), NEG_INF, F32)
    xs = jnp.concatenate([x.reshape(n_lat, D_MODEL), ctx.reshape(bsz * CTX_LEN, D_MODEL)], axis=0)

    for l in range(DEPTH):
        need_ctx = l < DEPTH - 1
        mod, gains = mod_all[l], norm_gains[l]
        w_a, w_b, w_c, w_d, w_m = _split_w_in(w_in[l])
        za, zb, zc, zd = _in_proj(xs, mod, gains, (w_a, w_b, w_c, w_d), rope, bsz)

        ya = _attn_a(za, a_sink[l], bsz)
        yb = _attn_b(zb, bias_cls[l], bsz)
        if need_ctx:
            ya = jnp.concatenate([ya, _attn_ctx(za, a_sink[l], bsz, A_HEADS, A_KV_HEADS)], axis=0)
            yb = jnp.concatenate([yb, _attn_ctx(zb, no_sink, bsz, B_HEADS, B_HEADS)], axis=0)
        ocf, ocb, odf, odb = _scan(zc, zd, lb_all[l].reshape(1, -1), d_gate_up[l],
                                   d_gate_bias[l].reshape(2, 1, -1), bsz)

        rows_l = n_rows if need_ctx else n_lat
        x_mid = _post(xs, mod, gains, ya, yb, ocf, ocb, odf, odb, zc, zd,
                      c_norm[l].reshape(1, -1), d_norm[l].reshape(1, -1),
                      w_m, w_branch[l].astype(BF16), w_out[l].astype(BF16), bsz, rows_l)
        xs = _ffn(x_mid, mod, gains, w_ff1[l].astype(BF16), w_ff2[l].astype(BF16), bsz, rows_l)
    return xs[:n_lat].reshape(bsz, SEQ, D_MODEL)
```

```python
import functools

import numpy as np
import jax
import jax.numpy as jnp
from jax import lax
from jax.experimental import pallas as pl
from jax.experimental.pallas import tpu as pltpu

D_MODEL = 1024
SEQ = 2048
DEPTH = 4
CTX_LEN = 256
GRID_W = 64
HEAD_DIM = 64
A_HEADS = 8
A_KV_HEADS = 2
A_WINDOW = 128
A_BLOCK = 128
B_HEADS = 8
NA_KH = 8
NA_KW = 16
C_HEADS = 4
C_DK = 128
D_HEADS = 4
D_DK = 64
D_GATE_RANK = 16
D_GATE_NORM = 16.0
N_BRANCH = 4
D_FF = 4 * D_MODEL
ROPE_BASE = 10000.0
EPS = 1e-6
NEG_INF = -1e30
BRANCH_W = 512

F32 = jnp.float32
BF16 = jnp.bfloat16

LANES = 128
TM = 512
TM_POST = 512
SUB_ROWS = 256
FF_CHUNK = 1024
SCAN_T = 64
SCAN_GROUP = 16
SCAN_SAFE_RANGE = 150.0
VMEM_LIMIT = 52 * 1024 * 1024

GD_WIDTH = 1664

_NT = (((1,), (1,)), ((), ()))
_TN = (((0,), (0,)), ((), ()))


def _cparams(sem):
    return pltpu.CompilerParams(dimension_semantics=sem, vmem_limit_bytes=VMEM_LIMIT)


def _const_spec(shape):
    return pl.BlockSpec(shape, lambda *_: (0,) * len(shape), pipeline_mode=pl.Buffered(1))


def _rms(x, gain):
    return x * lax.rsqrt(jnp.mean(x * x, axis=-1, keepdims=True) + EPS) * gain


def _norm_mod(x, gain, sc, sh):
    return _rms(x, gain) * (1.0 + sc) + sh


def _silu(x):
    return x * jax.nn.sigmoid(x)


def _dot_split(a, b):
    a_hi = a.astype(BF16)
    a_lo = (a - a_hi.astype(F32)).astype(BF16)
    b_hi = b.astype(BF16)
    b_lo = (b - b_hi.astype(F32)).astype(BF16)
    return (jnp.dot(a_hi, b_hi, preferred_element_type=F32) + jnp.dot(a_hi, b_lo, preferred_element_type=F32)
            + jnp.dot(a_lo, b_hi, preferred_element_type=F32))


def _log_sigmoid(x):
    return jnp.minimum(x, 0.0) - jnp.log1p(jnp.exp(-jnp.abs(x)))


def _mod_kernel(c_ref, w_ref, b_ref, o_ref):
    a = _silu(c_ref[...]).astype(BF16)
    o_ref[0] = jnp.dot(a, w_ref[0].astype(BF16), preferred_element_type=F32) + b_ref[0]


def _modulation(c_rows, w_mod, b_mod):
    rows = c_rows.shape[0]
    return pl.pallas_call(
        _mod_kernel,
        grid=(DEPTH, 6),
        in_specs=[pl.BlockSpec((rows, D_MODEL), lambda l, j: (0, 0)),
                  pl.BlockSpec((1, D_MODEL, D_MODEL), lambda l, j: (l, 0, j)),
                  pl.BlockSpec((1, 1, D_MODEL), lambda l, j: (l, 0, j))],
        out_specs=pl.BlockSpec((1, rows, D_MODEL), lambda l, j: (l, 0, j)),
        out_shape=jax.ShapeDtypeStruct((DEPTH, rows, 6 * D_MODEL), F32),
        compiler_params=_cparams(("parallel", "parallel")),
        name="modulation",
    )(c_rows, w_mod, b_mod.reshape(DEPTH, 1, 6 * D_MODEL))


ROPE_SLABS = 5


CW = C_HEADS * C_DK
DKW = D_HEADS * D_DK
ZCB_WIDTH = 5 * CW
ZDB_WIDTH = 2 * DKW + 2 * BRANCH_W


def _in_proj_kernel(x_ref, mod_ref, gain_ref, wa_ref, wb_ref, wc_ref, wd_ref, cos_ref, s1_ref, s2_ref,
                    lb_ref, up_ref, gb_ref, za_ref, zb_ref, zcb_ref, zcg_ref, zdb_ref, zdg_ref):
    m = mod_ref[0]
    lb = lb_ref[...]
    for r in range(TM // SUB_ROWS):
        rs = slice(r * SUB_ROWS, (r + 1) * SUB_ROWS)
        h = _norm_mod(x_ref[rs, :], gain_ref[0:1, :], m[1:2, :], m[0:1, :]).astype(BF16)
        z = jnp.dot(h, wa_ref[...], preferred_element_type=F32)
        cos, s1, s2 = cos_ref[rs, :], s1_ref[rs, :], s2_ref[rs, :]
        for s in range(ROPE_SLABS):
            zs = z[:, s * LANES:(s + 1) * LANES]
            rot = (zs * cos + pltpu.roll(zs, LANES - 16, axis=1) * s1
                   + pltpu.roll(zs, 16, axis=1) * s2)
            za_ref[rs, s * LANES:(s + 1) * LANES] = rot.astype(za_ref.dtype)
        za_ref[rs, ROPE_SLABS * LANES:] = z[:, ROPE_SLABS * LANES:].astype(za_ref.dtype)
        zb_ref[rs, :] = jnp.dot(h, wb_ref[...], preferred_element_type=F32).astype(zb_ref.dtype)
        z = jnp.dot(h, wc_ref[...], preferred_element_type=F32)
        zcb_ref[rs, 0:CW] = _silu(z[:, 0:CW]).astype(zcb_ref.dtype)
        for d in range(2):
            forget = jax.nn.sigmoid(z[:, (1 + d) * CW:(2 + d) * CW])
            zcg_ref[rs, d * CW:(d + 1) * CW] = jnp.log(lb + (1.0 - lb) * forget)
            key = (1.0 - lb) * (1.0 - forget)
            zcb_ref[rs, (1 + d) * CW:(2 + d) * CW] = key.astype(zcb_ref.dtype)
        zcb_ref[rs, 3 * CW:] = z[:, 3 * CW:].astype(zcb_ref.dtype)
        z = jnp.dot(h, wd_ref[...], preferred_element_type=F32)
        zdb_ref[rs, 0:DKW] = (z[:, 0:DKW] * (D_DK ** -0.5)).astype(zdb_ref.dtype)
        zdb_ref[rs, DKW:] = z[:, DKW:ZDB_WIDTH].astype(zdb_ref.dtype)
        for d in range(2):
            zg = z[:, ZDB_WIDTH + d * D_GATE_RANK:ZDB_WIDTH + (d + 1) * D_GATE_RANK]
            graw = _dot_split(zg, up_ref[d]) + gb_ref[d]
            zdg_ref[rs, d * DKW:(d + 1) * DKW] = _log_sigmoid(graw) / D_GATE_NORM


def _in_proj(xs, mod, gains, weights, rope, lb, gate_up, gate_bias, bsz):
    rows = xs.shape[0]
    n_lat_tiles = bsz * SEQ // TM
    tiles_per_seq = SEQ // TM

    def mod_idx(i):
        return (jnp.where(i < n_lat_tiles, i // tiles_per_seq, bsz), 0, 0)

    def rope_idx(i):
        return (jnp.where(i < n_lat_tiles, i % tiles_per_seq, tiles_per_seq), 0)

    outs = ((weights[0].shape[1], BF16), (weights[1].shape[1], BF16), (ZCB_WIDTH, BF16), (2 * CW, F32),
            (ZDB_WIDTH, BF16), (2 * DKW, F32))
    return pl.pallas_call(
        _in_proj_kernel,
        grid=(rows // TM,),
        in_specs=([pl.BlockSpec((TM, D_MODEL), lambda i: (i, 0)),
                   pl.BlockSpec((1, 8, D_MODEL), mod_idx),
                   _const_spec((4, D_MODEL))]
                  + [_const_spec(w.shape) for w in weights]
                  + [pl.BlockSpec((TM, LANES), rope_idx)] * 3
                  + [_const_spec(lb.shape), _const_spec(gate_up.shape), _const_spec(gate_bias.shape)]),
        out_specs=[pl.BlockSpec((TM, w), lambda i: (i, 0)) for w, _ in outs],
        out_shape=[jax.ShapeDtypeStruct((rows, w), dt) for w, dt in outs],
        compiler_params=_cparams(("parallel",)),
        name="in_proj",
    )(xs, mod, gains, *weights, *rope, lb, gate_up, gate_bias)


def _rope_tables():
    t = jnp.arange(SEQ)
    row = (t // GRID_W).astype(F32)
    col = (t % GRID_W).astype(F32)
    half = HEAD_DIM // 2
    inv = ROPE_BASE ** (-jnp.arange(0, half, 2, dtype=F32) / half)
    lane = np.arange(LANES)
    hl = lane % HEAD_DIM
    use_row = (hl // half) == 0
    freq = hl % (half // 2)
    first = (hl % half) < (half // 2)
    ang = jnp.where(use_row[None, :], row[:, None], col[:, None]) * inv[freq][None, :]
    cos, sin = jnp.cos(ang), jnp.sin(ang)
    s1 = jnp.where(first[None, :], -sin, 0.0)
    s2 = jnp.where(first[None, :], 0.0, sin)
    ident = jnp.ones((TM, LANES), F32)
    zero = jnp.zeros((TM, LANES), F32)
    return (jnp.concatenate([cos, ident]), jnp.concatenate([s1, zero]), jnp.concatenate([s2, zero]))


def _attn_a_kernel(sink_ref, q_ref, kp_ref, kc_ref, kn_ref, vp_ref, vc_ref, vn_ref, kx_ref, vx_ref, mask_ref,
                   o_ref):
    grp = A_HEADS // A_KV_HEADS
    rows = grp * A_BLOCK
    scale = HEAD_DIM ** -0.5
    mask = jnp.concatenate([mask_ref[0]] * grp, axis=0)
    head_of_row = lax.broadcasted_iota(jnp.int32, (rows, 1), 0) >> 7

    def scores(g):
        hs = slice(g * HEAD_DIM, (g + 1) * HEAD_DIM)
        q = jnp.concatenate([q_ref[:, (g * grp + j) * HEAD_DIM:(g * grp + j + 1) * HEAD_DIM]
                             for j in range(grp)], axis=0) * scale
        k = jnp.concatenate([kp_ref[:, hs], kc_ref[:, hs], kn_ref[:, hs], kx_ref[:, hs]], axis=0)
        return lax.dot_general(q, k, _NT, preferred_element_type=F32)

    all_scores = [scores(g) for g in range(A_KV_HEADS)]
    for g in range(A_KV_HEADS):
        hs = slice(g * HEAD_DIM, (g + 1) * HEAD_DIM)
        v = jnp.concatenate([vp_ref[:, hs], vc_ref[:, hs], vn_ref[:, hs], vx_ref[:, hs]], axis=0)
        s = all_scores[g] + mask
        sink = jnp.zeros((rows, 1), F32)
        for j in range(grp):
            sink = jnp.where(head_of_row == j, sink_ref[g * grp + j], sink)
        m = jnp.maximum(jnp.max(s, axis=-1, keepdims=True), sink)
        p = jnp.exp(s - m)
        den = jnp.sum(p, axis=-1, keepdims=True) + jnp.exp(sink - m)
        o = jnp.dot(p.astype(BF16), v, preferred_element_type=F32) / den
        o = jnp.concatenate([o[j * A_BLOCK:(j + 1) * A_BLOCK] for j in range(grp)], axis=1)
        o_ref[:, g * grp * HEAD_DIM:(g + 1) * grp * HEAD_DIM] = o.astype(o_ref.dtype)


def _window_masks():
    nb = SEQ // A_BLOCK
    qi = np.arange(A_BLOCK)[:, None]
    kj = np.arange(3 * A_BLOCK)[None, :]
    out = []
    for i in (0, 1, nb - 1):
        qpos = i * A_BLOCK + qi
        kpos = (i - 1) * A_BLOCK + kj
        ok = (np.abs(kpos - qpos) <= A_WINDOW) & (kpos >= 0) & (kpos < SEQ)
        ok = np.concatenate([ok, np.ones((A_BLOCK, CTX_LEN), bool)], axis=1)
        out.append(np.where(ok, 0.0, NEG_INF).astype(np.float32))
    return jnp.asarray(np.stack(out))


def _attn_a(za, sink, bsz):
    nb = SEQ // A_BLOCK
    qcb = A_HEADS * HEAD_DIM // LANES
    ctx_blk0 = bsz * SEQ // CTX_LEN
    nkeys = 3 * A_BLOCK + CTX_LEN

    def mask_idx(b, i):
        return (jnp.where(i == 0, 0, jnp.where(i == nb - 1, 2, 1)), 0, 0)

    def kv_spec(col, off):
        return pl.BlockSpec((A_BLOCK, LANES),
                            lambda b, i: (b * nb + jnp.clip(i + off, 0, nb - 1), col))

    return pl.pallas_call(
        _attn_a_kernel,
        grid=(bsz, nb),
        in_specs=[pl.BlockSpec(memory_space=pltpu.SMEM),
                  pl.BlockSpec((A_BLOCK, A_HEADS * HEAD_DIM), lambda b, i: (b * nb + i, 0)),
                  kv_spec(qcb, -1), kv_spec(qcb, 0), kv_spec(qcb, 1),
                  kv_spec(qcb + 1, -1), kv_spec(qcb + 1, 0), kv_spec(qcb + 1, 1),
                  pl.BlockSpec((CTX_LEN, LANES), lambda b, i: (ctx_blk0 + b, qcb)),
                  pl.BlockSpec((CTX_LEN, LANES), lambda b, i: (ctx_blk0 + b, qcb + 1)),
                  pl.BlockSpec((1, A_BLOCK, nkeys), mask_idx)],
        out_specs=pl.BlockSpec((A_BLOCK, BRANCH_W), lambda b, i: (b * nb + i, 0)),
        out_shape=jax.ShapeDtypeStruct((bsz * SEQ, BRANCH_W), BF16),
        compiler_params=_cparams(("parallel", "parallel")),
        name="attn_window",
    )(sink, za, za, za, za, za, za, za, za, za, _window_masks())


NA_QROWS = 4
NA_KROWS = 12
NA_QTOK = NA_QROWS * GRID_W
NA_KEYS = NA_KROWS * GRID_W + CTX_LEN


def _attn_b_kernel(q_ref, k0_ref, k1_ref, k2_ref, v0_ref, v1_ref, v2_ref, kx_ref, vx_ref, bias_ref, o_ref):
    scale = HEAD_DIM ** -0.5

    def scores(h):
        hs = slice(h * HEAD_DIM, (h + 1) * HEAD_DIM)
        q = q_ref[:, hs] * scale
        k = jnp.concatenate([k0_ref[:, hs], k1_ref[:, hs], k2_ref[:, hs], kx_ref[:, hs]], axis=0)
        return lax.dot_general(q, k, _NT, preferred_element_type=F32)

    outs = []
    s_next = scores(0)
    for h in range(B_HEADS):
        hs = slice(h * HEAD_DIM, (h + 1) * HEAD_DIM)
        s = s_next + bias_ref[0, h]
        v = jnp.concatenate([v0_ref[:, hs], v1_ref[:, hs], v2_ref[:, hs], vx_ref[:, hs]], axis=0)
        m = jnp.max(s, axis=-1, keepdims=True)
        p = jnp.exp(s - m)
        den = jnp.sum(p, axis=-1, keepdims=True)
        p = p.astype(BF16)
        if h + 1 < B_HEADS:
            s_next = scores(h + 1)
        outs.append(jnp.dot(p, v, preferred_element_type=F32) / den)
    o_ref[...] = jnp.concatenate(outs, axis=1).astype(o_ref.dtype)


def _bias_expand_kernel(rb_ref, onehot_ref, o_ref):
    o_ref[...] = jnp.dot(rb_ref[...], onehot_ref[...], preferred_element_type=F32,
                         precision=lax.Precision.HIGHEST)


def _na_bias_classes(rel_bias_all):
    rows = SEQ // GRID_W
    n_dr, n_dc = 2 * NA_KH - 1, 2 * NA_KW - 1
    qc = np.arange(GRID_W)[:, None]
    kc = np.arange(GRID_W)[None, :]
    cstart = np.clip(qc - NA_KW // 2, 0, GRID_W - NA_KW)
    col_ok = (kc >= cstart) & (kc < cstart + NA_KW)
    dc_idx = np.clip(kc - qc, -(NA_KW - 1), NA_KW - 1) + (NA_KW - 1)
    onehot = (np.arange(n_dc + 1)[:, None] == dc_idx.reshape(1, -1)).astype(np.float32)
    n_tab = DEPTH * B_HEADS * n_dr
    rb = jnp.pad(rel_bias_all.astype(F32).reshape(n_tab, n_dc), ((0, 0), (0, 1)))
    dense = pl.pallas_call(
        _bias_expand_kernel,
        grid=(DEPTH,),
        in_specs=[pl.BlockSpec((n_tab // DEPTH, n_dc + 1), lambda l: (l, 0)),
                  pl.BlockSpec((n_dc + 1, GRID_W * GRID_W), lambda l: (0, 0))],
        out_specs=pl.BlockSpec((n_tab // DEPTH, GRID_W * GRID_W), lambda l: (l, 0)),
        out_shape=jax.ShapeDtypeStruct((n_tab, GRID_W * GRID_W), F32),
        compiler_params=_cparams(("parallel",)),
        name="bias_expand",
    )(rb, jnp.asarray(onehot))
    dense = dense.reshape(DEPTH, B_HEADS, n_dr, GRID_W, GRID_W)
    masked = jnp.where(col_ok[None, None, None], dense, NEG_INF)
    n_groups = rows // NA_QROWS
    out = []
    for g in (0, 1, n_groups - 1):
        base = _na_key_base(g)
        per_row = []
        for lr in range(NA_QROWS):
            r = g * NA_QROWS + lr
            r0 = min(max(r - NA_KH // 2, 0), rows - NA_KH)
            dr_lo = r0 - r + (NA_KH - 1)
            before = r0 - base
            piece = jnp.pad(masked[:, :, dr_lo:dr_lo + NA_KH],
                            ((0, 0), (0, 0), (before, NA_KROWS - NA_KH - before), (0, 0), (0, 0)),
                            constant_values=NEG_INF)
            per_row.append(piece.transpose(0, 1, 3, 2, 4).reshape(DEPTH, B_HEADS, GRID_W, NA_KROWS * GRID_W))
        cls = jnp.concatenate(per_row, axis=2)
        out.append(jnp.pad(cls, ((0, 0), (0, 0), (0, 0), (0, CTX_LEN))))
    return jnp.stack(out, axis=1)


def _na_key_base(g):
    rows = SEQ // GRID_W
    return min(max(g * NA_QROWS - NA_KH // 2, 0), rows - NA_KROWS)


def _attn_b(zb, bias_cls, bsz):
    n_groups = SEQ // NA_QTOK
    ctx_blk0 = bsz * SEQ // CTX_LEN
    max_base_blk = _na_key_base(n_groups - 1) // NA_QROWS

    def cls_idx(g, b):
        return (jnp.where(g == 0, 0, jnp.where(g == n_groups - 1, 2, 1)), 0, 0, 0)

    def kv_spec(col, j):
        return pl.BlockSpec((NA_QTOK, BRANCH_W),
                            lambda g, b: (b * n_groups + jnp.clip(g - 1, 0, max_base_blk) + j, col))

    return pl.pallas_call(
        _attn_b_kernel,
        grid=(n_groups, bsz),
        in_specs=[pl.BlockSpec((NA_QTOK, BRANCH_W), lambda g, b: (b * n_groups + g, 0)),
                  kv_spec(1, 0), kv_spec(1, 1), kv_spec(1, 2),
                  kv_spec(2, 0), kv_spec(2, 1), kv_spec(2, 2),
                  pl.BlockSpec((CTX_LEN, BRANCH_W), lambda g, b: (ctx_blk0 + b, 1)),
                  pl.BlockSpec((CTX_LEN, BRANCH_W), lambda g, b: (ctx_blk0 + b, 2)),
                  pl.BlockSpec((1, B_HEADS, NA_QTOK, NA_KEYS), cls_idx)],
        out_specs=pl.BlockSpec((NA_QTOK, BRANCH_W), lambda g, b: (b * n_groups + g, 0)),
        out_shape=jax.ShapeDtypeStruct((bsz * SEQ, BRANCH_W), BF16),
        compiler_params=_cparams(("arbitrary", "arbitrary")),
        name="attn_neighbourhood",
    )(zb, zb, zb, zb, zb, zb, zb, zb, zb, bias_cls)


def _attn_ctx_kernel(sink_ref, q_ref, k_ref, v_ref, o_ref, *, n_heads, n_kv):
    grp = n_heads // n_kv
    rows = grp * CTX_LEN
    scale = HEAD_DIM ** -0.5
    head_of_row = lax.broadcasted_iota(jnp.int32, (rows, 1), 0) >> 8
    for g in range(n_kv):
        hs = slice(g * HEAD_DIM, (g + 1) * HEAD_DIM)
        q = jnp.concatenate([q_ref[:, (g * grp + j) * HEAD_DIM:(g * grp + j + 1) * HEAD_DIM]
                             for j in range(grp)], axis=0)
        s = lax.dot_general(q, k_ref[:, hs], _NT, preferred_element_type=F32) * scale
        sink = jnp.zeros((rows, 1), F32)
        for j in range(grp):
            sink = jnp.where(head_of_row == j, sink_ref[g * grp + j], sink)
        m = jnp.maximum(jnp.max(s, axis=-1, keepdims=True), sink)
        p = jnp.exp(s - m)
        den = jnp.sum(p, axis=-1, keepdims=True) + jnp.exp(sink - m)
        o = jnp.dot(p.astype(BF16), v_ref[:, hs], preferred_element_type=F32) / den
        o = jnp.concatenate([o[j * CTX_LEN:(j + 1) * CTX_LEN] for j in range(grp)], axis=1)
        o_ref[:, g * grp * HEAD_DIM:(g + 1) * grp * HEAD_DIM] = o.astype(o_ref.dtype)


def _attn_ctx(z, sink, bsz, n_heads, n_kv):
    ctx_blk0 = bsz * SEQ // CTX_LEN
    kvw = n_kv * HEAD_DIM
    kcol = n_heads * HEAD_DIM // kvw
    return pl.pallas_call(
        functools.partial(_attn_ctx_kernel, n_heads=n_heads, n_kv=n_kv),
        grid=(bsz,),
        in_specs=[pl.BlockSpec(memory_space=pltpu.SMEM),
                  pl.BlockSpec((CTX_LEN, n_heads * HEAD_DIM), lambda b: (ctx_blk0 + b, 0)),
                  pl.BlockSpec((CTX_LEN, kvw), lambda b: (ctx_blk0 + b, kcol)),
                  pl.BlockSpec((CTX_LEN, kvw), lambda b: (ctx_blk0 + b, kcol + 1))],
        out_specs=pl.BlockSpec((CTX_LEN, BRANCH_W), lambda b: (b, 0)),
        out_shape=jax.ShapeDtypeStruct((bsz * CTX_LEN, BRANCH_W), BF16),
        compiler_params=_cparams(("parallel",)),
        name="attn_ctx_%d" % n_kv,
    )(sink, z, z, z)


def _cumulative_decay(g, tri):
    trib = tri.astype(BF16)
    g_hi = g.astype(BF16)
    rem = g - g_hi.astype(F32)
    g_mid = rem.astype(BF16)
    g_lo = (rem - g_mid.astype(F32)).astype(BF16)
    return (jnp.dot(trib, g_hi, preferred_element_type=F32) + jnp.dot(trib, g_mid, preferred_element_type=F32)
            + jnp.dot(trib, g_lo, preferred_element_type=F32))


def _scan_chunks(chains, cums, tris):
    def scaled_operands(chain, cum):
        q, k, v, state, d, n_heads, dk = chain
        t = q.shape[0]
        tot = cum[t - 1:t] if d == 0 else cum[0:1]
        mid = 0.5 * tot
        e_mid = jnp.exp(mid)
        qa = q * jnp.exp(cum - mid)
        kb = k * jnp.exp(mid - cum)
        qd = (qa * e_mid).astype(BF16)
        kd = (kb * e_mid).astype(BF16)
        return (qa.astype(BF16), kb.astype(BF16), qd, kd, v.astype(BF16),
                state * jnp.exp(tot), state.astype(BF16))

    def unmasked_matmuls(chain, ops):
        q, k, v, state, d, n_heads, dk = chain
        qa, kb, qd, kd, vb, decayed, state_b = ops
        per_head = []
        for h in range(n_heads):
            ks = slice(h * dk, (h + 1) * dk)
            vs = slice(h * LANES, (h + 1) * LANES)
            a = lax.dot_general(qa[:, ks], kb[:, ks], _NT, preferred_element_type=F32)
            carry = lax.dot_general(qd[:, ks], state_b[:, ks], _NT, preferred_element_type=F32)
            upd = lax.dot_general(vb[:, vs], kd[:, ks], _TN, preferred_element_type=F32)
            per_head.append((a, carry, decayed[:, ks] + upd))
        return per_head

    def masked_matmuls(chain, ops, per_head):
        d, vb = chain[4], ops[4]
        outs = []
        for h, (a, carry, new_s) in enumerate(per_head):
            a = jnp.where(tris[d], a, 0.0).astype(BF16)
            outs.append(jnp.dot(a, vb[:, h * LANES:(h + 1) * LANES], preferred_element_type=F32) + carry)
        return jnp.concatenate(outs, axis=1), jnp.concatenate([p[2] for p in per_head], axis=1)

    n = len(chains)
    ops, partial, results = [None] * n, [None] * n, [None] * n
    for i in range(n + 1):
        if i < n:
            ops[i] = scaled_operands(chains[i], cums[i])
            partial[i] = unmasked_matmuls(chains[i], ops[i])
        if i > 0:
            results[i - 1] = masked_matmuls(chains[i - 1], ops[i - 1], partial[i - 1])
    return results


def _scan_tokens(ref_chains):
    n_groups = SCAN_T // SCAN_GROUP
    rows = lax.broadcasted_iota(jnp.int32, (SCAN_GROUP, 1), 0)

    def group_body(gi, carry):
        for (q_ref, q_off, k_ref, k_off, v_ref, g_ref, o_ref, s_ref, d, n_heads, dk) in ref_chains:
            w = n_heads * dk
            g_idx = gi if d == 0 else n_groups - 1 - gi
            base = pl.multiple_of(g_idx * SCAN_GROUP, SCAN_GROUP)
            q16 = q_ref[pl.ds(base, SCAN_GROUP), q_off:q_off + w].astype(F32)
            k16 = k_ref[pl.ds(base, SCAN_GROUP), k_off:k_off + w].astype(F32)
            v16 = v_ref[pl.ds(base, SCAN_GROUP), :].astype(F32)
            g16 = g_ref[pl.ds(base, SCAN_GROUP), :]

            def token_body(jj, out16, q16=q16, k16=k16, v16=v16, g16=g16, s_ref=s_ref, d=d, n_heads=n_heads,
                           dk=dk):
                j = jj if d == 0 else SCAN_GROUP - 1 - jj
                sel = rows == j
                decay = jnp.exp(jnp.sum(jnp.where(sel, g16, 0.0), axis=0, keepdims=True))
                s = s_ref[...] * decay
                kj = jnp.where(sel, k16, 0.0)
                qj = jnp.where(sel, q16, 0.0)
                outs, cols = [], []
                for h in range(n_heads):
                    ks = slice(h * dk, (h + 1) * dk)
                    vs = slice(h * LANES, (h + 1) * LANES)
                    s_h = s[:, ks] + lax.dot_general(v16[:, vs], kj[:, ks], _TN, preferred_element_type=F32)
                    cols.append(s_h)
                    outs.append(lax.dot_general(qj[:, ks], s_h, _NT, preferred_element_type=F32))
                s_ref[...] = jnp.concatenate(cols, axis=1)
                return out16 + jnp.concatenate(outs, axis=1)

            out16 = lax.fori_loop(0, SCAN_GROUP, token_body, jnp.zeros((SCAN_GROUP, BRANCH_W), F32))
            o_ref[pl.ds(base, SCAN_GROUP), :] = out16
        return carry

    lax.fori_loop(0, n_groups, group_body, 0)


def _scan_kernel(cq_f, ck_f, ci_f, cg_f, cq_b, ck_b, ci_b, cg_b, dqk_f, dv_f, dg_f, dqk_b, dv_b, dg_b,
                 ocf_ref, ocb_ref, odf_ref, odb_ref, scf_scr, scb_scr, sdf_scr, sdb_scr):
    scratches = (scf_scr, scb_scr, sdf_scr, sdb_scr)

    @pl.when(pl.program_id(1) == 0)
    def _():
        for s in scratches:
            s[...] = jnp.zeros_like(s)

    dirs = ((cq_f, ck_f, ci_f, cg_f, dqk_f, dv_f, dg_f), (cq_b, ck_b, ci_b, cg_b, dqk_b, dv_b, dg_b))
    total = jnp.zeros((1, LANES), F32)
    for g_ref in (cg_f, cg_b, dg_f, dg_b):
        col_tot = jnp.sum(g_ref[...], axis=0, keepdims=True)
        for j in range(col_tot.shape[1] // LANES):
            total = jnp.minimum(total, col_tot[:, j * LANES:(j + 1) * LANES])
    in_range = jnp.min(total) > -SCAN_SAFE_RANGE

    t = SCAN_T
    row = lax.broadcasted_iota(jnp.int32, (t, t), 0)
    col = lax.broadcasted_iota(jnp.int32, (t, t), 1)
    tris = (col <= row, col >= row)
    cums = []
    for d, (cq, ck, ci, cg, dqk, dv, dg) in enumerate(dirs):
        cums += [_cumulative_decay(cg[...], tris[d]), _cumulative_decay(dg[...], tris[d])]

    @pl.when(in_range)
    def _():
        states = [s[...] for s in scratches]
        chains = []
        for d, (cq, ck, ci, cg, dqk, dv, dg) in enumerate(dirs):
            chains.append((cq[...].astype(F32), ck[...].astype(F32), ci[...], states[d], d, C_HEADS, C_DK))
            qk = dqk[...].astype(F32)
            chains.append((qk[:, :DKW], qk[:, DKW:], dv[...], states[2 + d], d, D_HEADS, D_DK))
        (ocf, scf), (odf, sdf), (ocb, scb), (odb, sdb) = _scan_chunks(chains, cums, tris)
        for ref, val in ((ocf_ref, ocf), (ocb_ref, ocb), (odf_ref, odf), (odb_ref, odb),
                         (scf_scr, scf), (scb_scr, scb), (sdf_scr, sdf), (sdb_scr, sdb)):
            ref[...] = val

    @pl.when(jnp.logical_not(in_range))
    def _():
        ref_chains = []
        for d, (cq, ck, ci, cg, dqk, dv, dg) in enumerate(dirs):
            ref_chains.append((cq, 0, ck, 0, ci, cg, (ocf_ref, ocb_ref)[d], scratches[d], d, C_HEADS, C_DK))
            ref_chains.append((dqk, 0, dqk, DKW, dv, dg, (odf_ref, odb_ref)[d], scratches[2 + d], d,
                               D_HEADS, D_DK))
        _scan_tokens(ref_chains)


def _scan(zcb, zcg, zdb, zdg, bsz):
    t = SCAN_T
    n_ctx = CTX_LEN // t
    n_lat = SEQ // t
    ctx_blk0 = bsz * n_lat

    def fwd(b, c):
        return jnp.where(c < n_ctx, ctx_blk0 + b * n_ctx + c, b * n_lat + c - n_ctx)

    def bwd(b, c):
        return jnp.where(c < n_ctx, ctx_blk0 + b * n_ctx + (n_ctx - 1 - c), b * n_lat + (n_lat + n_ctx - 1 - c))

    def spec(order, width, col):
        return pl.BlockSpec((t, width), lambda b, c: (order(b, c), col))

    in_specs = [spec(fwd, CW, 0), spec(fwd, CW, 1), spec(fwd, CW, 3), spec(fwd, CW, 0),
                spec(bwd, CW, 0), spec(bwd, CW, 2), spec(bwd, CW, 3), spec(bwd, CW, 1),
                spec(fwd, 2 * DKW, 0), spec(fwd, BRANCH_W, 1), spec(fwd, DKW, 0),
                spec(bwd, 2 * DKW, 0), spec(bwd, BRANCH_W, 1), spec(bwd, DKW, 1)]
    rows = zcb.shape[0]
    out = jax.ShapeDtypeStruct((rows, BRANCH_W), F32)
    return pl.pallas_call(
        _scan_kernel,
        grid=(bsz, n_ctx + n_lat),
        in_specs=in_specs,
        out_specs=[spec(fwd, 512, 0), spec(bwd, 512, 0), spec(fwd, 512, 0), spec(bwd, 512, 0)],
        out_shape=[out, out, out, out],
        scratch_shapes=[pltpu.VMEM((LANES, C_HEADS * C_DK), F32), pltpu.VMEM((LANES, C_HEADS * C_DK), F32),
                        pltpu.VMEM((LANES, D_HEADS * D_DK), F32), pltpu.VMEM((LANES, D_HEADS * D_DK), F32)],
        compiler_params=_cparams(("parallel", "arbitrary")),
        name="bidir_scan",
    )(zcb, zcb, zcb, zcg, zcb, zcb, zcb, zcg, zdb, zdb, zdg, zdb, zdb, zdg)


def _post_kernel(x_ref, mod_ref, gain_ref, ya_ref, yb_ref, ocf_ref, ocb_ref, odf_ref, odb_ref,
                 cg_ref, dg_ref, cn_ref, dn_ref, wm_ref, wb_ref, wo_ref, o_ref):
    m = mod_ref[0]

    def gated_group_norm(o, gain, gate):
        parts = [_rms(o[:, j * LANES:(j + 1) * LANES], gain) for j in range(BRANCH_W // LANES)]
        return (jnp.concatenate(parts, axis=1) * _silu(gate.astype(F32))).astype(BF16)

    for r in range(TM_POST // SUB_ROWS):
        rs = slice(r * SUB_ROWS, (r + 1) * SUB_ROWS)
        x = x_ref[rs, :]
        h = _norm_mod(x, gain_ref[0:1, :], m[1:2, :], m[0:1, :]).astype(BF16)
        y_c = gated_group_norm(ocf_ref[rs, :] + ocb_ref[rs, :], cn_ref[...], cg_ref[rs, :])
        y_d = gated_group_norm(odf_ref[rs, :] + odb_ref[rs, :], dn_ref[...], dg_ref[rs, :])
        branches = (ya_ref[rs, :], yb_ref[rs, :], y_c, y_d)
        acc = None
        for j, yb in enumerate(branches):
            gate = jnp.dot(h, wm_ref[:, j * D_MODEL:(j + 1) * D_MODEL], preferred_element_type=F32)
            term = jax.nn.sigmoid(gate) * jnp.dot(yb, wb_ref[j], preferred_element_type=F32)
            acc = term if acc is None else acc + term
        y = jnp.dot(acc.astype(BF16), wo_ref[...], preferred_element_type=F32)
        o_ref[rs, :] = x + m[2:3, :] * _rms(y, gain_ref[1:2, :])


def _post(xs, mod, gains, ya, yb, ocf, ocb, odf, odb, zc, zd, c_norm, d_norm, wm, wb, wo, bsz, n_rows):
    tm = TM_POST
    n_lat_tiles = bsz * SEQ // tm
    tiles_per_seq = SEQ // tm

    def mod_idx(i):
        return (jnp.where(i < n_lat_tiles, i // tiles_per_seq, bsz), 0, 0)

    row = lambda w, col=0: pl.BlockSpec((tm, w), lambda i: (i, col))
    const = _const_spec
    return pl.pallas_call(
        _post_kernel,
        grid=(n_rows // tm,),
        in_specs=[row(D_MODEL), pl.BlockSpec((1, 8, D_MODEL), mod_idx), const((4, D_MODEL)),
                  row(BRANCH_W), row(BRANCH_W), row(BRANCH_W), row(BRANCH_W), row(BRANCH_W), row(BRANCH_W),
                  row(BRANCH_W, 4), row(BRANCH_W, 2),
                  const((1, LANES)), const((1, LANES)),
                  const((D_MODEL, N_BRANCH * D_MODEL)), const((N_BRANCH, BRANCH_W, D_MODEL)),
                  const((D_MODEL, D_MODEL))],
        out_specs=row(D_MODEL),
        out_shape=jax.ShapeDtypeStruct((n_rows, D_MODEL), F32),
        compiler_params=_cparams(("parallel",)),
        name="merge_out",
    )(xs, mod, gains, ya, yb, ocf, ocb, odf, odb, zc, zd, c_norm, d_norm, wm, wb, wo)


def _ffn_kernel(x_ref, mod_ref, gain_ref, w1_ref, w2_ref, o_ref):
    m = mod_ref[0]
    for r in range(TM // SUB_ROWS):
        rs = slice(r * SUB_ROWS, (r + 1) * SUB_ROWS)
        x = x_ref[rs, :]
        h = _norm_mod(x, gain_ref[2:3, :], m[4:5, :], m[3:4, :]).astype(BF16)
        acc = None
        for k in range(D_FF // FF_CHUNK):
            ks = slice(k * FF_CHUNK, (k + 1) * FF_CHUNK)
            u = jnp.dot(h, w1_ref[:, ks], preferred_element_type=F32)
            u = jnp.square(jnp.maximum(u, 0.0)).astype(BF16)
            part = jnp.dot(u, w2_ref[ks, :], preferred_element_type=F32)
            acc = part if acc is None else acc + part
        o_ref[rs, :] = x + m[5:6, :] * _rms(acc, gain_ref[3:4, :])


def _ffn(xs, mod, gains, w1, w2, bsz, n_rows):
    n_lat_tiles = bsz * SEQ // TM
    tiles_per_seq = SEQ // TM

    def mod_idx(i):
        return (jnp.where(i < n_lat_tiles, i // tiles_per_seq, bsz), 0, 0)

    return pl.pallas_call(
        _ffn_kernel,
        grid=(n_rows // TM,),
        in_specs=[pl.BlockSpec((TM, D_MODEL), lambda i: (i, 0)),
                  pl.BlockSpec((1, 8, D_MODEL), mod_idx),
                  _const_spec((4, D_MODEL)),
                  _const_spec((D_MODEL, D_FF)),
                  _const_spec((D_FF, D_MODEL))],
        out_specs=pl.BlockSpec((TM, D_MODEL), lambda i: (i, 0)),
        out_shape=jax.ShapeDtypeStruct((n_rows, D_MODEL), F32),
        compiler_params=_cparams(("parallel",)),
        name="mlp",
    )(xs, mod, gains, w1, w2)


def _split_w_in(w_in_l):
    a_w = A_HEADS * HEAD_DIM + 2 * A_KV_HEADS * HEAD_DIM
    b_w = 3 * B_HEADS * HEAD_DIM
    c_w = 5 * C_HEADS * C_DK
    o_a, o_b, o_c = 0, a_w, a_w + b_w
    o_d = o_c + c_w
    kw = D_HEADS * D_DK
    d_q = w_in_l[:, o_d:o_d + kw]
    d_k = w_in_l[:, o_d + kw:o_d + 2 * kw]
    d_v = w_in_l[:, o_d + 2 * kw:o_d + 2 * kw + 512]
    o_gk = o_d + 2 * kw + 512
    d_gk = w_in_l[:, o_gk:o_gk + 2 * D_GATE_RANK]
    d_g = w_in_l[:, o_gk + 2 * D_GATE_RANK:o_gk + 2 * D_GATE_RANK + 512]
    o_m = o_gk + 2 * D_GATE_RANK + 512
    pad = jnp.zeros((D_MODEL, LANES - 2 * D_GATE_RANK), w_in_l.dtype)
    g_d = jnp.concatenate([d_q, d_k, d_v, d_g, d_gk, pad], axis=1)
    groups = (w_in_l[:, o_a:o_b], w_in_l[:, o_b:o_c], w_in_l[:, o_c:o_d], g_d, w_in_l[:, o_m:])
    return tuple(g.astype(BF16) for g in groups)


def kernel(x, c, ctx, c_ctx, w_mod, b_mod, norm_gains, w_in, a_sink, b_rel_bias, c_lower_bounds, c_norm,
           d_gate_up, d_gate_bias, d_norm, w_branch, w_out, w_ff1, w_ff2):
    bsz = x.shape[0]
    n_lat = bsz * SEQ
    n_rows = n_lat + bsz * CTX_LEN

    lb_soft = jax.nn.softmax(c_lower_bounds.astype(F32), axis=0)
    lb_all = jnp.cumsum(lb_soft, axis=0) - lb_soft[0:1]

    c_rows = jnp.concatenate([c, c_ctx[None, :], jnp.zeros((-(bsz + 1) % 8, D_MODEL), F32)], axis=0)
    mod_all = _modulation(c_rows, w_mod, b_mod)[:, :bsz + 1]
    mod_all = mod_all.reshape(DEPTH, bsz + 1, 6, D_MODEL)
    mod_all = jnp.pad(mod_all, ((0, 0), (0, 0), (0, 2), (0, 0)))

    rope = _rope_tables()
    bias_cls = _na_bias_classes(b_rel_bias)
    no_sink = jnp.full((B_HEADS,), NEG_INF, F32)
    xs = jnp.concatenate([x.reshape(n_lat, D_MODEL), ctx.reshape(bsz * CTX_LEN, D_MODEL)], axis=0)

    for l in range(DEPTH):
        need_ctx = l < DEPTH - 1
        mod, gains = mod_all[l], norm_gains[l]
        w_a, w_b, w_c, w_d, w_m = _split_w_in(w_in[l])
        za, zb, zcb, zcg, zdb, zdg = _in_proj(xs, mod, gains, (w_a, w_b, w_c, w_d), rope,
                                              lb_all[l].reshape(1, -1), d_gate_up[l],
                                              d_gate_bias[l].reshape(2, 1, -1), bsz)

        ya = _attn_a(za, a_sink[l], bsz)
        yb = _attn_b(zb, bias_cls[l], bsz)
        if need_ctx:
            ya = jnp.concatenate([ya, _attn_ctx(za, a_sink[l], bsz, A_HEADS, A_KV_HEADS)], axis=0)
            yb = jnp.concatenate([yb, _attn_ctx(zb, no_sink, bsz, B_HEADS, B_HEADS)], axis=0)
        ocf, ocb, odf, odb = _scan(zcb, zcg, zdb, zdg, bsz)

        rows_l = n_rows if need_ctx else n_lat
        x_mid = _post(xs, mod, gains, ya, yb, ocf, ocb, odf, odb, zcb, zdb,
                      c_norm[l].reshape(1, -1), d_norm[l].reshape(1, -1),
                      w_m, w_branch[l].astype(BF16), w_out[l].astype(BF16), bsz, rows_l)
        xs = _ffn(x_mid, mod, gains, w_ff1[l].astype(BF16), w_ff2[l].astype(BF16), bsz, rows_l)
    return xs[:n_lat].reshape(bsz, SEQ, D_MODEL)
```

```python
import functools

import numpy as np
import jax
import jax.numpy as jnp
from jax import lax
from jax.experimental import pallas as pl
from jax.experimental.pallas import tpu as pltpu

D_MODEL = 1024
SEQ = 2048
DEPTH = 4
CTX_LEN = 256
GRID_W = 64
HEAD_DIM = 64
A_HEADS = 8
A_KV_HEADS = 2
A_WINDOW = 128
A_BLOCK = 128
B_HEADS = 8
NA_KH = 8
NA_KW = 16
C_HEADS = 4
C_DK = 128
D_HEADS = 4
D_DK = 64
D_GATE_RANK = 16
D_GATE_NORM = 16.0
N_BRANCH = 4
D_FF = 4 * D_MODEL
ROPE_BASE = 10000.0
EPS = 1e-6
NEG_INF = -1e30
BRANCH_W = 512

F32 = jnp.float32
BF16 = jnp.bfloat16

LANES = 128
TM = 512
TM_POST = 512
SUB_ROWS = 256
FF_CHUNK = 1024
SCAN_T = 64
SCAN_GROUP = 16
SCAN_SAFE_RANGE = 150.0
VMEM_LIMIT = 52 * 1024 * 1024

GD_WIDTH = 1664

_NT = (((1,), (1,)), ((), ()))
_TN = (((0,), (0,)), ((), ()))


def _cparams(sem):
    return pltpu.CompilerParams(dimension_semantics=sem, vmem_limit_bytes=VMEM_LIMIT)


def _const_spec(shape):
    return pl.BlockSpec(shape, lambda *_: (0,) * len(shape), pipeline_mode=pl.Buffered(1))


def _rms(x, gain):
    return x * lax.rsqrt(jnp.mean(x * x, axis=-1, keepdims=True) + EPS) * gain


def _norm_mod(x, gain, sc, sh):
    return _rms(x, gain) * (1.0 + sc) + sh


def _silu(x):
    return x * jax.nn.sigmoid(x)


def _log_sigmoid(x):
    return jnp.minimum(x, 0.0) - jnp.log1p(jnp.exp(-jnp.abs(x)))


def _mod_kernel(c_ref, w_ref, b_ref, o_ref):
    a = _silu(c_ref[...]).astype(BF16)
    o_ref[0] = jnp.dot(a, w_ref[0].astype(BF16), preferred_element_type=F32) + b_ref[0]


def _modulation(c_rows, w_mod, b_mod):
    rows = c_rows.shape[0]
    return pl.pallas_call(
        _mod_kernel,
        grid=(DEPTH, 6),
        in_specs=[pl.BlockSpec((rows, D_MODEL), lambda l, j: (0, 0)),
                  pl.BlockSpec((1, D_MODEL, D_MODEL), lambda l, j: (l, 0, j)),
                  pl.BlockSpec((1, 1, D_MODEL), lambda l, j: (l, 0, j))],
        out_specs=pl.BlockSpec((1, rows, D_MODEL), lambda l, j: (l, 0, j)),
        out_shape=jax.ShapeDtypeStruct((DEPTH, rows, 6 * D_MODEL), F32),
        compiler_params=_cparams(("parallel", "parallel")),
        name="modulation",
    )(c_rows, w_mod, b_mod.reshape(DEPTH, 1, 6 * D_MODEL))


ROPE_SLABS = 5


CW = C_HEADS * C_DK
DKW = D_HEADS * D_DK
ZCB_WIDTH = 5 * CW
ZDB_WIDTH = 2 * DKW + 2 * BRANCH_W

W_IN_A = (0, A_HEADS * HEAD_DIM + 2 * A_KV_HEADS * HEAD_DIM)
W_IN_B = (W_IN_A[1], W_IN_A[1] + 3 * B_HEADS * HEAD_DIM)
W_IN_C = (W_IN_B[1], W_IN_B[1] + 5 * CW)
W_IN_D = (W_IN_C[1], W_IN_C[1] + GD_WIDTH)
W_IN_MERGE = (W_IN_C[1] + ZDB_WIDTH + 2 * D_GATE_RANK, W_IN_C[1] + ZDB_WIDTH + 2 * D_GATE_RANK + N_BRANCH * D_MODEL)


def _in_proj_kernel(x_ref, mod_ref, gain_ref, w_ref, cos_ref, s1_ref, s2_ref,
                    lb_ref, up_ref, gb_ref, za_ref, zb_ref, zcb_ref, zcg_ref, zdb_ref, zdg_ref):
    m = mod_ref[0]
    lb = lb_ref[...]
    wa_ref = w_ref.at[:, W_IN_A[0]:W_IN_A[1]]
    wb_ref = w_ref.at[:, W_IN_B[0]:W_IN_B[1]]
    wc_ref = w_ref.at[:, W_IN_C[0]:W_IN_C[1]]
    wd_ref = w_ref.at[:, W_IN_D[0]:W_IN_D[1]]
    gk0 = 2 * DKW + BRANCH_W
    dg0 = gk0 + 2 * D_GATE_RANK
    for r in range(TM // SUB_ROWS):
        rs = slice(r * SUB_ROWS, (r + 1) * SUB_ROWS)
        h = _norm_mod(x_ref[rs, :], gain_ref[0:1, :], m[1:2, :], m[0:1, :]).astype(BF16)
        z = jnp.dot(h, wa_ref[...], preferred_element_type=F32)
        cos, s1, s2 = cos_ref[rs, :], s1_ref[rs, :], s2_ref[rs, :]
        for s in range(ROPE_SLABS):
            zs = z[:, s * LANES:(s + 1) * LANES]
            rot = (zs * cos + pltpu.roll(zs, LANES - 16, axis=1) * s1
                   + pltpu.roll(zs, 16, axis=1) * s2)
            za_ref[rs, s * LANES:(s + 1) * LANES] = rot.astype(za_ref.dtype)
        za_ref[rs, ROPE_SLABS * LANES:] = z[:, ROPE_SLABS * LANES:].astype(za_ref.dtype)
        zb_ref[rs, :] = jnp.dot(h, wb_ref[...], preferred_element_type=F32).astype(zb_ref.dtype)
        z = jnp.dot(h, wc_ref[...], preferred_element_type=F32)
        zcb_ref[rs, 0:CW] = _silu(z[:, 0:CW]).astype(zcb_ref.dtype)
        for d in range(2):
            forget = jax.nn.sigmoid(z[:, (1 + d) * CW:(2 + d) * CW])
            zcg_ref[rs, d * CW:(d + 1) * CW] = jnp.log(lb + (1.0 - lb) * forget)
            key = (1.0 - lb) * (1.0 - forget)
            zcb_ref[rs, (1 + d) * CW:(2 + d) * CW] = key.astype(zcb_ref.dtype)
        zcb_ref[rs, 3 * CW:] = z[:, 3 * CW:].astype(zcb_ref.dtype)
        z = jnp.dot(h, wd_ref[...], preferred_element_type=F32)
        zdb_ref[rs, 0:DKW] = (z[:, 0:DKW] * (D_DK ** -0.5)).astype(zdb_ref.dtype)
        zdb_ref[rs, DKW:gk0] = z[:, DKW:gk0].astype(zdb_ref.dtype)
        zdb_ref[rs, gk0:] = z[:, dg0:dg0 + BRANCH_W].astype(zdb_ref.dtype)
        for d in range(2):
            zg = z[:, gk0 + d * D_GATE_RANK:gk0 + (d + 1) * D_GATE_RANK].astype(BF16)
            graw = jnp.dot(zg, up_ref[d].astype(BF16), preferred_element_type=F32) + gb_ref[d]
            zdg_ref[rs, d * DKW:(d + 1) * DKW] = _log_sigmoid(graw) / D_GATE_NORM


def _in_proj(xs, mod, gains, w, rope, lb, gate_up, gate_bias, bsz):
    rows = xs.shape[0]
    n_lat_tiles = bsz * SEQ // TM
    tiles_per_seq = SEQ // TM

    def mod_idx(i):
        return (jnp.where(i < n_lat_tiles, i // tiles_per_seq, bsz), 0, 0)

    def rope_idx(i):
        return (jnp.where(i < n_lat_tiles, i % tiles_per_seq, tiles_per_seq), 0)

    outs = ((W_IN_A[1] - W_IN_A[0], BF16), (W_IN_B[1] - W_IN_B[0], BF16), (ZCB_WIDTH, BF16), (2 * CW, F32),
            (ZDB_WIDTH, BF16), (2 * DKW, F32))
    return pl.pallas_call(
        _in_proj_kernel,
        grid=(rows // TM,),
        in_specs=([pl.BlockSpec((TM, D_MODEL), lambda i: (i, 0)),
                   pl.BlockSpec((1, 8, D_MODEL), mod_idx),
                   _const_spec((4, D_MODEL)),
                   _const_spec(w.shape)]
                  + [pl.BlockSpec((TM, LANES), rope_idx)] * 3
                  + [_const_spec(lb.shape), _const_spec(gate_up.shape), _const_spec(gate_bias.shape)]),
        out_specs=[pl.BlockSpec((TM, width), lambda i: (i, 0)) for width, _ in outs],
        out_shape=[jax.ShapeDtypeStruct((rows, width), dt) for width, dt in outs],
        compiler_params=_cparams(("parallel",)),
        name="in_proj",
    )(xs, mod, gains, w, *rope, lb, gate_up, gate_bias)


def _rope_tables():
    t = jnp.arange(SEQ)
    row = (t // GRID_W).astype(F32)
    col = (t % GRID_W).astype(F32)
    half = HEAD_DIM // 2
    inv = ROPE_BASE ** (-jnp.arange(0, half, 2, dtype=F32) / half)
    lane = np.arange(LANES)
    hl = lane % HEAD_DIM
    use_row = (hl // half) == 0
    freq = hl % (half // 2)
    first = (hl % half) < (half // 2)
    ang = jnp.where(use_row[None, :], row[:, None], col[:, None]) * inv[freq][None, :]
    cos, sin = jnp.cos(ang), jnp.sin(ang)
    s1 = jnp.where(first[None, :], -sin, 0.0)
    s2 = jnp.where(first[None, :], 0.0, sin)
    ident = jnp.ones((TM, LANES), F32)
    zero = jnp.zeros((TM, LANES), F32)
    return (jnp.concatenate([cos, ident]), jnp.concatenate([s1, zero]), jnp.concatenate([s2, zero]))


def _attn_a_kernel(sink_ref, q_ref, kp_ref, kc_ref, kn_ref, vp_ref, vc_ref, vn_ref, kx_ref, vx_ref, mask_ref,
                   o_ref):
    grp = A_HEADS // A_KV_HEADS
    rows = grp * A_BLOCK
    scale = HEAD_DIM ** -0.5
    mask = jnp.concatenate([mask_ref[0]] * grp, axis=0)
    head_of_row = lax.broadcasted_iota(jnp.int32, (rows, 1), 0) >> 7

    def scores(g):
        hs = slice(g * HEAD_DIM, (g + 1) * HEAD_DIM)
        q = jnp.concatenate([q_ref[:, (g * grp + j) * HEAD_DIM:(g * grp + j + 1) * HEAD_DIM]
                             for j in range(grp)], axis=0) * scale
        k = jnp.concatenate([kp_ref[:, hs], kc_ref[:, hs], kn_ref[:, hs], kx_ref[:, hs]], axis=0)
        return lax.dot_general(q, k, _NT, preferred_element_type=F32)

    all_scores = [scores(g) for g in range(A_KV_HEADS)]
    ones = jnp.ones((3 * A_BLOCK + CTX_LEN, HEAD_DIM), BF16)
    for g in range(A_KV_HEADS):
        hs = slice(g * HEAD_DIM, (g + 1) * HEAD_DIM)
        v = jnp.concatenate([vp_ref[:, hs], vc_ref[:, hs], vn_ref[:, hs], vx_ref[:, hs]], axis=0)
        s = all_scores[g] + mask
        sink = jnp.zeros((rows, 1), F32)
        for j in range(grp):
            sink = jnp.where(head_of_row == j, sink_ref[g * grp + j], sink)
        m = jnp.maximum(jnp.max(s, axis=-1, keepdims=True), sink)
        p = jnp.exp(s - m).astype(BF16)
        ov = jnp.dot(p, jnp.concatenate([v, ones], axis=1), preferred_element_type=F32)
        o = ov[:, :HEAD_DIM] / (ov[:, HEAD_DIM:] + jnp.exp(sink - m))
        o = jnp.concatenate([o[j * A_BLOCK:(j + 1) * A_BLOCK] for j in range(grp)], axis=1)
        o_ref[:, g * grp * HEAD_DIM:(g + 1) * grp * HEAD_DIM] = o.astype(o_ref.dtype)


def _window_masks():
    nb = SEQ // A_BLOCK
    qi = np.arange(A_BLOCK)[:, None]
    kj = np.arange(3 * A_BLOCK)[None, :]
    out = []
    for i in (0, 1, nb - 1):
        qpos = i * A_BLOCK + qi
        kpos = (i - 1) * A_BLOCK + kj
        ok = (np.abs(kpos - qpos) <= A_WINDOW) & (kpos >= 0) & (kpos < SEQ)
        ok = np.concatenate([ok, np.ones((A_BLOCK, CTX_LEN), bool)], axis=1)
        out.append(np.where(ok, 0.0, NEG_INF).astype(np.float32))
    return jnp.asarray(np.stack(out))


def _attn_a(za, sink, bsz, out_rows):
    nb = SEQ // A_BLOCK
    qcb = A_HEADS * HEAD_DIM // LANES
    ctx_blk0 = bsz * SEQ // CTX_LEN
    nkeys = 3 * A_BLOCK + CTX_LEN

    def mask_idx(b, i):
        return (jnp.where(i == 0, 0, jnp.where(i == nb - 1, 2, 1)), 0, 0)

    def kv_spec(col, off):
        return pl.BlockSpec((A_BLOCK, LANES),
                            lambda b, i: (b * nb + jnp.clip(i + off, 0, nb - 1), col))

    return pl.pallas_call(
        _attn_a_kernel,
        grid=(bsz, nb),
        in_specs=[pl.BlockSpec(memory_space=pltpu.SMEM),
                  pl.BlockSpec((A_BLOCK, A_HEADS * HEAD_DIM), lambda b, i: (b * nb + i, 0)),
                  kv_spec(qcb, -1), kv_spec(qcb, 0), kv_spec(qcb, 1),
                  kv_spec(qcb + 1, -1), kv_spec(qcb + 1, 0), kv_spec(qcb + 1, 1),
                  pl.BlockSpec((CTX_LEN, LANES), lambda b, i: (ctx_blk0 + b, qcb)),
                  pl.BlockSpec((CTX_LEN, LANES), lambda b, i: (ctx_blk0 + b, qcb + 1)),
                  pl.BlockSpec((1, A_BLOCK, nkeys), mask_idx)],
        out_specs=pl.BlockSpec((A_BLOCK, BRANCH_W), lambda b, i: (b * nb + i, 0)),
        out_shape=jax.ShapeDtypeStruct((out_rows, BRANCH_W), BF16),
        compiler_params=_cparams(("parallel", "parallel")),
        name="attn_window",
    )(sink, za, za, za, za, za, za, za, za, za, _window_masks())


NA_QROWS = 4
NA_KROWS = 12
NA_QTOK = NA_QROWS * GRID_W
NA_KEYS = NA_KROWS * GRID_W + CTX_LEN


def _attn_b_kernel(q_ref, k0_ref, k1_ref, k2_ref, v0_ref, v1_ref, v2_ref, kx_ref, vx_ref, bias_ref, o_ref):
    scale = HEAD_DIM ** -0.5

    def scores(h):
        hs = slice(h * HEAD_DIM, (h + 1) * HEAD_DIM)
        q = q_ref[:, hs] * scale
        k = jnp.concatenate([k0_ref[:, hs], k1_ref[:, hs], k2_ref[:, hs], kx_ref[:, hs]], axis=0)
        return lax.dot_general(q, k, _NT, preferred_element_type=F32)

    outs = []
    ones = jnp.ones((NA_KEYS, HEAD_DIM), BF16)
    s_next = scores(0)
    for h in range(B_HEADS):
        hs = slice(h * HEAD_DIM, (h + 1) * HEAD_DIM)
        s = s_next + bias_ref[0, h]
        v = jnp.concatenate([v0_ref[:, hs], v1_ref[:, hs], v2_ref[:, hs], vx_ref[:, hs]], axis=0)
        m = jnp.max(s, axis=-1, keepdims=True)
        p = jnp.exp(s - m).astype(BF16)
        if h + 1 < B_HEADS:
            s_next = scores(h + 1)
        ov = jnp.dot(p, jnp.concatenate([v, ones], axis=1), preferred_element_type=F32)
        outs.append(ov[:, :HEAD_DIM] / ov[:, HEAD_DIM:])
    o_ref[...] = jnp.concatenate(outs, axis=1).astype(o_ref.dtype)


def _bias_expand_kernel(rb_ref, onehot_ref, o_ref):
    o_ref[...] = jnp.dot(rb_ref[...], onehot_ref[...], preferred_element_type=F32,
                         precision=lax.Precision.HIGHEST)


def _na_bias_classes(rel_bias_all):
    rows = SEQ // GRID_W
    n_dr, n_dc = 2 * NA_KH - 1, 2 * NA_KW - 1
    qc = np.arange(GRID_W)[:, None]
    kc = np.arange(GRID_W)[None, :]
    cstart = np.clip(qc - NA_KW // 2, 0, GRID_W - NA_KW)
    col_ok = (kc >= cstart) & (kc < cstart + NA_KW)
    dc_idx = np.clip(kc - qc, -(NA_KW - 1), NA_KW - 1) + (NA_KW - 1)
    onehot = (np.arange(n_dc + 1)[:, None] == dc_idx.reshape(1, -1)).astype(np.float32)
    n_tab = DEPTH * B_HEADS * n_dr
    rb = jnp.pad(rel_bias_all.astype(F32).reshape(n_tab, n_dc), ((0, 0), (0, 1)))
    dense = pl.pallas_call(
        _bias_expand_kernel,
        grid=(DEPTH,),
        in_specs=[pl.BlockSpec((n_tab // DEPTH, n_dc + 1), lambda l: (l, 0)),
                  pl.BlockSpec((n_dc + 1, GRID_W * GRID_W), lambda l: (0, 0))],
        out_specs=pl.BlockSpec((n_tab // DEPTH, GRID_W * GRID_W), lambda l: (l, 0)),
        out_shape=jax.ShapeDtypeStruct((n_tab, GRID_W * GRID_W), F32),
        compiler_params=_cparams(("parallel",)),
        name="bias_expand",
    )(rb, jnp.asarray(onehot))
    dense = dense.reshape(DEPTH, B_HEADS, n_dr, GRID_W, GRID_W)
    masked = jnp.where(col_ok[None, None, None], dense, NEG_INF)
    n_groups = rows // NA_QROWS
    out = []
    for g in (0, 1, n_groups - 1):
        base = _na_key_base(g)
        per_row = []
        for lr in range(NA_QROWS):
            r = g * NA_QROWS + lr
            r0 = min(max(r - NA_KH // 2, 0), rows - NA_KH)
            dr_lo = r0 - r + (NA_KH - 1)
            before = r0 - base
            piece = jnp.pad(masked[:, :, dr_lo:dr_lo + NA_KH],
                            ((0, 0), (0, 0), (before, NA_KROWS - NA_KH - before), (0, 0), (0, 0)),
                            constant_values=NEG_INF)
            per_row.append(piece.transpose(0, 1, 3, 2, 4).reshape(DEPTH, B_HEADS, GRID_W, NA_KROWS * GRID_W))
        cls = jnp.concatenate(per_row, axis=2)
        out.append(jnp.pad(cls, ((0, 0), (0, 0), (0, 0), (0, CTX_LEN))))
    return jnp.stack(out, axis=1)


def _na_key_base(g):
    rows = SEQ // GRID_W
    return min(max(g * NA_QROWS - NA_KH // 2, 0), rows - NA_KROWS)


def _attn_b(zb, bias_cls, bsz, out_rows):
    n_groups = SEQ // NA_QTOK
    ctx_blk0 = bsz * SEQ // CTX_LEN
    max_base_blk = _na_key_base(n_groups - 1) // NA_QROWS

    def cls_idx(g, b):
        return (jnp.where(g == 0, 0, jnp.where(g == n_groups - 1, 2, 1)), 0, 0, 0)

    def kv_spec(col, j):
        return pl.BlockSpec((NA_QTOK, BRANCH_W),
                            lambda g, b: (b * n_groups + jnp.clip(g - 1, 0, max_base_blk) + j, col))

    return pl.pallas_call(
        _attn_b_kernel,
        grid=(n_groups, bsz),
        in_specs=[pl.BlockSpec((NA_QTOK, BRANCH_W), lambda g, b: (b * n_groups + g, 0)),
                  kv_spec(1, 0), kv_spec(1, 1), kv_spec(1, 2),
                  kv_spec(2, 0), kv_spec(2, 1), kv_spec(2, 2),
                  pl.BlockSpec((CTX_LEN, BRANCH_W), lambda g, b: (ctx_blk0 + b, 1)),
                  pl.BlockSpec((CTX_LEN, BRANCH_W), lambda g, b: (ctx_blk0 + b, 2)),
                  pl.BlockSpec((1, B_HEADS, NA_QTOK, NA_KEYS), cls_idx)],
        out_specs=pl.BlockSpec((NA_QTOK, BRANCH_W), lambda g, b: (b * n_groups + g, 0)),
        out_shape=jax.ShapeDtypeStruct((out_rows, BRANCH_W), BF16),
        compiler_params=_cparams(("arbitrary", "arbitrary")),
        name="attn_neighbourhood",
    )(zb, zb, zb, zb, zb, zb, zb, zb, zb, bias_cls)


def _attn_ctx_kernel(sink_ref, q_ref, k_ref, v_ref, lat_ref, o_ref, *, n_heads, n_kv):
    del lat_ref
    grp = n_heads // n_kv
    rows = grp * CTX_LEN
    scale = HEAD_DIM ** -0.5
    head_of_row = lax.broadcasted_iota(jnp.int32, (rows, 1), 0) >> 8
    for g in range(n_kv):
        hs = slice(g * HEAD_DIM, (g + 1) * HEAD_DIM)
        q = jnp.concatenate([q_ref[:, (g * grp + j) * HEAD_DIM:(g * grp + j + 1) * HEAD_DIM]
                             for j in range(grp)], axis=0)
        s = lax.dot_general(q, k_ref[:, hs], _NT, preferred_element_type=F32) * scale
        sink = jnp.zeros((rows, 1), F32)
        for j in range(grp):
            sink = jnp.where(head_of_row == j, sink_ref[g * grp + j], sink)
        m = jnp.maximum(jnp.max(s, axis=-1, keepdims=True), sink)
        p = jnp.exp(s - m)
        den = jnp.sum(p, axis=-1, keepdims=True) + jnp.exp(sink - m)
        o = jnp.dot(p.astype(BF16), v_ref[:, hs], preferred_element_type=F32) / den
        o = jnp.concatenate([o[j * CTX_LEN:(j + 1) * CTX_LEN] for j in range(grp)], axis=1)
        o_ref[:, g * grp * HEAD_DIM:(g + 1) * grp * HEAD_DIM] = o.astype(o_ref.dtype)


def _attn_ctx(z, sink, y_lat, bsz, n_heads, n_kv):
    ctx_blk0 = bsz * SEQ // CTX_LEN
    kvw = n_kv * HEAD_DIM
    kcol = n_heads * HEAD_DIM // kvw
    return pl.pallas_call(
        functools.partial(_attn_ctx_kernel, n_heads=n_heads, n_kv=n_kv),
        grid=(bsz,),
        in_specs=[pl.BlockSpec(memory_space=pltpu.SMEM),
                  pl.BlockSpec((CTX_LEN, n_heads * HEAD_DIM), lambda b: (ctx_blk0 + b, 0)),
                  pl.BlockSpec((CTX_LEN, kvw), lambda b: (ctx_blk0 + b, kcol)),
                  pl.BlockSpec((CTX_LEN, kvw), lambda b: (ctx_blk0 + b, kcol + 1)),
                  pl.BlockSpec(memory_space=pl.ANY)],
        out_specs=pl.BlockSpec((CTX_LEN, BRANCH_W), lambda b: (ctx_blk0 + b, 0)),
        out_shape=jax.ShapeDtypeStruct(y_lat.shape, y_lat.dtype),
        input_output_aliases={4: 0},
        compiler_params=_cparams(("parallel",)),
        name="attn_ctx_%d" % n_kv,
    )(sink, z, z, z, y_lat)


def _cumulative_decay(g, tri):
    trib = tri.astype(BF16)
    g_hi = g.astype(BF16)
    rem = g - g_hi.astype(F32)
    g_mid = rem.astype(BF16)
    g_lo = (rem - g_mid.astype(F32)).astype(BF16)
    return (jnp.dot(trib, g_hi, preferred_element_type=F32) + jnp.dot(trib, g_mid, preferred_element_type=F32)
            + jnp.dot(trib, g_lo, preferred_element_type=F32))


def _scan_chunks(chains, cums, tris):
    def scaled_operands(chain, cum):
        q, k, v, state, d, n_heads, dk = chain
        t = q.shape[0]
        tot = cum[t - 1:t] if d == 0 else cum[0:1]
        mid = 0.5 * tot
        e_mid = jnp.exp(mid)
        qa = q * jnp.exp(cum - mid)
        kb = k * jnp.exp(mid - cum)
        qd = (qa * e_mid).astype(BF16)
        kd = (kb * e_mid).astype(BF16)
        return (qa.astype(BF16), kb.astype(BF16), qd, kd, v.astype(BF16),
                state * jnp.exp(tot), state.astype(BF16))

    def unmasked_matmuls(chain, ops):
        q, k, v, state, d, n_heads, dk = chain
        qa, kb, qd, kd, vb, decayed, state_b = ops
        per_head = []
        for h in range(n_heads):
            ks = slice(h * dk, (h + 1) * dk)
            vs = slice(h * LANES, (h + 1) * LANES)
            a = lax.dot_general(qa[:, ks], kb[:, ks], _NT, preferred_element_type=F32)
            carry = lax.dot_general(qd[:, ks], state_b[:, ks], _NT, preferred_element_type=F32)
            upd = lax.dot_general(vb[:, vs], kd[:, ks], _TN, preferred_element_type=F32)
            per_head.append((a, carry, decayed[:, ks] + upd))
        return per_head

    def masked_matmuls(chain, ops, per_head):
        d, vb = chain[4], ops[4]
        outs = []
        for h, (a, carry, new_s) in enumerate(per_head):
            a = jnp.where(tris[d], a, 0.0).astype(BF16)
            outs.append(jnp.dot(a, vb[:, h * LANES:(h + 1) * LANES], preferred_element_type=F32) + carry)
        return jnp.concatenate(outs, axis=1), jnp.concatenate([p[2] for p in per_head], axis=1)

    n = len(chains)
    ops, partial, results = [None] * n, [None] * n, [None] * n
    for i in range(n + 1):
        if i < n:
            ops[i] = scaled_operands(chains[i], cums[i])
            partial[i] = unmasked_matmuls(chains[i], ops[i])
        if i > 0:
            results[i - 1] = masked_matmuls(chains[i - 1], ops[i - 1], partial[i - 1])
    return results


def _scan_tokens(ref_chains):
    n_groups = SCAN_T // SCAN_GROUP
    rows = lax.broadcasted_iota(jnp.int32, (SCAN_GROUP, 1), 0)

    def group_body(gi, carry):
        for (q_ref, q_off, k_ref, k_off, v_ref, g_ref, o_ref, s_ref, d, n_heads, dk) in ref_chains:
            w = n_heads * dk
            g_idx = gi if d == 0 else n_groups - 1 - gi
            base = pl.multiple_of(g_idx * SCAN_GROUP, SCAN_GROUP)
            q16 = q_ref[pl.ds(base, SCAN_GROUP), q_off:q_off + w].astype(F32)
            k16 = k_ref[pl.ds(base, SCAN_GROUP), k_off:k_off + w].astype(F32)
            v16 = v_ref[pl.ds(base, SCAN_GROUP), :].astype(F32)
            g16 = g_ref[pl.ds(base, SCAN_GROUP), :]

            def token_body(jj, out16, q16=q16, k16=k16, v16=v16, g16=g16, s_ref=s_ref, d=d, n_heads=n_heads,
                           dk=dk):
                j = jj if d == 0 else SCAN_GROUP - 1 - jj
                sel = rows == j
                decay = jnp.exp(jnp.sum(jnp.where(sel, g16, 0.0), axis=0, keepdims=True))
                s = s_ref[...] * decay
                kj = jnp.where(sel, k16, 0.0)
                qj = jnp.where(sel, q16, 0.0)
                outs, cols = [], []
                for h in range(n_heads):
                    ks = slice(h * dk, (h + 1) * dk)
                    vs = slice(h * LANES, (h + 1) * LANES)
                    s_h = s[:, ks] + lax.dot_general(v16[:, vs], kj[:, ks], _TN, preferred_element_type=F32)
                    cols.append(s_h)
                    outs.append(lax.dot_general(qj[:, ks], s_h, _NT, preferred_element_type=F32))
                s_ref[...] = jnp.concatenate(cols, axis=1)
                return out16 + jnp.concatenate(outs, axis=1)

            out16 = lax.fori_loop(0, SCAN_GROUP, token_body, jnp.zeros((SCAN_GROUP, BRANCH_W), F32))
            o_ref[pl.ds(base, SCAN_GROUP), :] = out16
        return carry

    lax.fori_loop(0, n_groups, group_body, 0)


def _scan_kernel(cq_f, ck_f, ci_f, cg_f, cq_b, ck_b, ci_b, cg_b, dqk_f, dv_f, dg_f, dqk_b, dv_b, dg_b,
                 ocf_ref, ocb_ref, odf_ref, odb_ref, scf_scr, scb_scr, sdf_scr, sdb_scr):
    scratches = (scf_scr, scb_scr, sdf_scr, sdb_scr)

    @pl.when(pl.program_id(1) == 0)
    def _():
        for s in scratches:
            s[...] = jnp.zeros_like(s)

    dirs = ((cq_f, ck_f, ci_f, cg_f, dqk_f, dv_f, dg_f), (cq_b, ck_b, ci_b, cg_b, dqk_b, dv_b, dg_b))
    total = jnp.zeros((1, LANES), F32)
    for g_ref in (cg_f, cg_b, dg_f, dg_b):
        col_tot = jnp.sum(g_ref[...], axis=0, keepdims=True)
        for j in range(col_tot.shape[1] // LANES):
            total = jnp.minimum(total, col_tot[:, j * LANES:(j + 1) * LANES])
    in_range = jnp.min(total) > -SCAN_SAFE_RANGE

    t = SCAN_T
    row = lax.broadcasted_iota(jnp.int32, (t, t), 0)
    col = lax.broadcasted_iota(jnp.int32, (t, t), 1)
    tris = (col <= row, col >= row)
    cums = []
    for d, (cq, ck, ci, cg, dqk, dv, dg) in enumerate(dirs):
        cums += [_cumulative_decay(cg[...], tris[d]), _cumulative_decay(dg[...], tris[d])]

    @pl.when(in_range)
    def _():
        states = [s[...] for s in scratches]
        chains = []
        for d, (cq, ck, ci, cg, dqk, dv, dg) in enumerate(dirs):
            chains.append((cq[...].astype(F32), ck[...].astype(F32), ci[...], states[d], d, C_HEADS, C_DK))
            qk = dqk[...].astype(F32)
            chains.append((qk[:, :DKW], qk[:, DKW:], dv[...], states[2 + d], d, D_HEADS, D_DK))
        (ocf, scf), (odf, sdf), (ocb, scb), (odb, sdb) = _scan_chunks(chains, cums, tris)
        for ref, val in ((ocf_ref, ocf), (ocb_ref, ocb), (odf_ref, odf), (odb_ref, odb),
                         (scf_scr, scf), (scb_scr, scb), (sdf_scr, sdf), (sdb_scr, sdb)):
            ref[...] = val

    @pl.when(jnp.logical_not(in_range))
    def _():
        ref_chains = []
        for d, (cq, ck, ci, cg, dqk, dv, dg) in enumerate(dirs):
            ref_chains.append((cq, 0, ck, 0, ci, cg, (ocf_ref, ocb_ref)[d], scratches[d], d, C_HEADS, C_DK))
            ref_chains.append((dqk, 0, dqk, DKW, dv, dg, (odf_ref, odb_ref)[d], scratches[2 + d], d,
                               D_HEADS, D_DK))
        _scan_tokens(ref_chains)


def _scan(zcb, zcg, zdb, zdg, bsz):
    t = SCAN_T
    n_ctx = CTX_LEN // t
    n_lat = SEQ // t
    ctx_blk0 = bsz * n_lat

    def fwd(b, c):
        return jnp.where(c < n_ctx, ctx_blk0 + b * n_ctx + c, b * n_lat + c - n_ctx)

    def bwd(b, c):
        return jnp.where(c < n_ctx, ctx_blk0 + b * n_ctx + (n_ctx - 1 - c), b * n_lat + (n_lat + n_ctx - 1 - c))

    def spec(order, width, col):
        return pl.BlockSpec((t, width), lambda b, c: (order(b, c), col))

    in_specs = [spec(fwd, CW, 0), spec(fwd, CW, 1), spec(fwd, CW, 3), spec(fwd, CW, 0),
                spec(bwd, CW, 0), spec(bwd, CW, 2), spec(bwd, CW, 3), spec(bwd, CW, 1),
                spec(fwd, 2 * DKW, 0), spec(fwd, BRANCH_W, 1), spec(fwd, DKW, 0),
                spec(bwd, 2 * DKW, 0), spec(bwd, BRANCH_W, 1), spec(bwd, DKW, 1)]
    rows = zcb.shape[0]
    out = jax.ShapeDtypeStruct((rows, BRANCH_W), F32)
    return pl.pallas_call(
        _scan_kernel,
        grid=(bsz, n_ctx + n_lat),
        in_specs=in_specs,
        out_specs=[spec(fwd, 512, 0), spec(bwd, 512, 0), spec(fwd, 512, 0), spec(bwd, 512, 0)],
        out_shape=[out, out, out, out],
        scratch_shapes=[pltpu.VMEM((LANES, C_HEADS * C_DK), F32), pltpu.VMEM((LANES, C_HEADS * C_DK), F32),
                        pltpu.VMEM((LANES, D_HEADS * D_DK), F32), pltpu.VMEM((LANES, D_HEADS * D_DK), F32)],
        compiler_params=_cparams(("parallel", "arbitrary")),
        name="bidir_scan",
    )(zcb, zcb, zcb, zcg, zcb, zcb, zcb, zcg, zdb, zdb, zdg, zdb, zdb, zdg)


def _post_kernel(x_ref, mod_ref, gain_ref, ya_ref, yb_ref, ocf_ref, ocb_ref, odf_ref, odb_ref,
                 cg_ref, dg_ref, cn_ref, dn_ref, wm_ref, wb_ref, wo_ref, o_ref):
    m = mod_ref[0]

    def gated_group_norm(o, gain, gate):
        parts = [_rms(o[:, j * LANES:(j + 1) * LANES], gain) for j in range(BRANCH_W // LANES)]
        return (jnp.concatenate(parts, axis=1) * _silu(gate.astype(F32))).astype(BF16)

    for r in range(TM_POST // SUB_ROWS):
        rs = slice(r * SUB_ROWS, (r + 1) * SUB_ROWS)
        x = x_ref[rs, :]
        h = _norm_mod(x, gain_ref[0:1, :], m[1:2, :], m[0:1, :]).astype(BF16)
        y_c = gated_group_norm(ocf_ref[rs, :] + ocb_ref[rs, :], cn_ref[...], cg_ref[rs, :])
        y_d = gated_group_norm(odf_ref[rs, :] + odb_ref[rs, :], dn_ref[...], dg_ref[rs, :])
        branches = (ya_ref[rs, :], yb_ref[rs, :], y_c, y_d)
        acc = None
        for j, yb in enumerate(branches):
            gate = jnp.dot(h, wm_ref[:, j * D_MODEL:(j + 1) * D_MODEL], preferred_element_type=F32)
            term = jax.nn.sigmoid(gate) * jnp.dot(yb, wb_ref[j], preferred_element_type=F32)
            acc = term if acc is None else acc + term
        y = jnp.dot(acc.astype(BF16), wo_ref[...], preferred_element_type=F32)
        o_ref[rs, :] = x + m[2:3, :] * _rms(y, gain_ref[1:2, :])


def _post(xs, mod, gains, ya, yb, ocf, ocb, odf, odb, zc, zd, c_norm, d_norm, wm, wb, wo, bsz, n_rows):
    tm = TM_POST
    n_lat_tiles = bsz * SEQ // tm
    tiles_per_seq = SEQ // tm

    def mod_idx(i):
        return (jnp.where(i < n_lat_tiles, i // tiles_per_seq, bsz), 0, 0)

    row = lambda w, col=0: pl.BlockSpec((tm, w), lambda i: (i, col))
    const = _const_spec
    return pl.pallas_call(
        _post_kernel,
        grid=(n_rows // tm,),
        in_specs=[row(D_MODEL), pl.BlockSpec((1, 8, D_MODEL), mod_idx), const((4, D_MODEL)),
                  row(BRANCH_W), row(BRANCH_W), row(BRANCH_W), row(BRANCH_W), row(BRANCH_W), row(BRANCH_W),
                  row(BRANCH_W, 4), row(BRANCH_W, 2),
                  const((1, LANES)), const((1, LANES)),
                  const((D_MODEL, N_BRANCH * D_MODEL)), const((N_BRANCH, BRANCH_W, D_MODEL)),
                  const((D_MODEL, D_MODEL))],
        out_specs=row(D_MODEL),
        out_shape=jax.ShapeDtypeStruct((n_rows, D_MODEL), F32),
        compiler_params=_cparams(("parallel",)),
        name="merge_out",
    )(xs, mod, gains, ya, yb, ocf, ocb, odf, odb, zc, zd, c_norm, d_norm, wm, wb, wo)


def _ffn_kernel(x_ref, mod_ref, gain_ref, w1_ref, w2_ref, o_ref):
    m = mod_ref[0]
    for r in range(TM // SUB_ROWS):
        rs = slice(r * SUB_ROWS, (r + 1) * SUB_ROWS)
        x = x_ref[rs, :]
        h = _norm_mod(x, gain_ref[2:3, :], m[4:5, :], m[3:4, :]).astype(BF16)
        acc = None
        for k in range(D_FF // FF_CHUNK):
            ks = slice(k * FF_CHUNK, (k + 1) * FF_CHUNK)
            u = jnp.dot(h, w1_ref[:, ks], preferred_element_type=F32)
            u = jnp.square(jnp.maximum(u, 0.0)).astype(BF16)
            part = jnp.dot(u, w2_ref[ks, :], preferred_element_type=F32)
            acc = part if acc is None else acc + part
        o_ref[rs, :] = x + m[5:6, :] * _rms(acc, gain_ref[3:4, :])


def _ffn(xs, mod, gains, w1, w2, bsz, n_rows):
    n_lat_tiles = bsz * SEQ // TM
    tiles_per_seq = SEQ // TM

    def mod_idx(i):
        return (jnp.where(i < n_lat_tiles, i // tiles_per_seq, bsz), 0, 0)

    return pl.pallas_call(
        _ffn_kernel,
        grid=(n_rows // TM,),
        in_specs=[pl.BlockSpec((TM, D_MODEL), lambda i: (i, 0)),
                  pl.BlockSpec((1, 8, D_MODEL), mod_idx),
                  _const_spec((4, D_MODEL)),
                  _const_spec((D_MODEL, D_FF)),
                  _const_spec((D_FF, D_MODEL))],
        out_specs=pl.BlockSpec((TM, D_MODEL), lambda i: (i, 0)),
        out_shape=jax.ShapeDtypeStruct((n_rows, D_MODEL), F32),
        compiler_params=_cparams(("parallel",)),
        name="mlp",
    )(xs, mod, gains, w1, w2)


def kernel(x, c, ctx, c_ctx, w_mod, b_mod, norm_gains, w_in, a_sink, b_rel_bias, c_lower_bounds, c_norm,
           d_gate_up, d_gate_bias, d_norm, w_branch, w_out, w_ff1, w_ff2):
    bsz = x.shape[0]
    n_lat = bsz * SEQ
    n_rows = n_lat + bsz * CTX_LEN

    lb_soft = jax.nn.softmax(c_lower_bounds.astype(F32), axis=0)
    lb_all = jnp.cumsum(lb_soft, axis=0) - lb_soft[0:1]

    c_rows = jnp.concatenate([c, c_ctx[None, :], jnp.zeros((-(bsz + 1) % 8, D_MODEL), F32)], axis=0)
    mod_all = _modulation(c_rows, w_mod, b_mod)[:, :bsz + 1]
    mod_all = mod_all.reshape(DEPTH, bsz + 1, 6, D_MODEL)
    mod_all = jnp.pad(mod_all, ((0, 0), (0, 0), (0, 2), (0, 0)))

    rope = _rope_tables()
    bias_cls = _na_bias_classes(b_rel_bias)
    no_sink = jnp.full((B_HEADS,), NEG_INF, F32)
    xs = jnp.concatenate([x.reshape(n_lat, D_MODEL), ctx.reshape(bsz * CTX_LEN, D_MODEL)], axis=0)

    for l in range(DEPTH):
        need_ctx = l < DEPTH - 1
        mod, gains = mod_all[l], norm_gains[l]
        w_proj = w_in[l, :, :W_IN_D[1]].astype(BF16)
        w_m = w_in[l, :, W_IN_MERGE[0]:W_IN_MERGE[1]].astype(BF16)
        za, zb, zcb, zcg, zdb, zdg = _in_proj(xs, mod, gains, w_proj, rope,
                                              lb_all[l].reshape(1, -1), d_gate_up[l],
                                              d_gate_bias[l].reshape(2, 1, -1), bsz)

        rows_l = n_rows if need_ctx else n_lat
        ya = _attn_a(za, a_sink[l], bsz, rows_l)
        yb = _attn_b(zb, bias_cls[l], bsz, rows_l)
        if need_ctx:
            ya = _attn_ctx(za, a_sink[l], ya, bsz, A_HEADS, A_KV_HEADS)
            yb = _attn_ctx(zb, no_sink, yb, bsz, B_HEADS, B_HEADS)
        ocf, ocb, odf, odb = _scan(zcb, zcg, zdb, zdg, bsz)

        x_mid = _post(xs, mod, gains, ya, yb, ocf, ocb, odf, odb, zcb, zdb,
                      c_norm[l].reshape(1, -1), d_norm[l].reshape(1, -1),
                      w_m, w_branch[l].astype(BF16), w_out[l].astype(BF16), bsz, rows_l)
        xs = _ffn(x_mid, mod, gains, w_ff1[l].astype(BF16), w_ff2[l].astype(BF16), bsz, rows_l)
    return xs[:n_lat].reshape(bsz, SEQ, D_MODEL)
```

```python
import functools

import numpy as np
import jax
import jax.numpy as jnp
from jax import lax
from jax.experimental import pallas as pl
from jax.experimental.pallas import tpu as pltpu

D_MODEL = 1024
SEQ = 2048
DEPTH = 4
CTX_LEN = 256
GRID_W = 64
HEAD_DIM = 64
A_HEADS = 8
A_KV_HEADS = 2
A_WINDOW = 128
A_BLOCK = 128
B_HEADS = 8
NA_KH = 8
NA_KW = 16
C_HEADS = 4
C_DK = 128
D_HEADS = 4
D_DK = 64
D_GATE_RANK = 16
D_GATE_NORM = 16.0
N_BRANCH = 4
D_FF = 4 * D_MODEL
ROPE_BASE = 10000.0
EPS = 1e-6
NEG_INF = -1e30
BRANCH_W = 512

F32 = jnp.float32
BF16 = jnp.bfloat16

LANES = 128
TM = 512
TM_POST = 512
SUB_ROWS = 256
FF_CHUNK = 1024
SCAN_T = 64
SCAN_GROUP = 16
SCAN_SAFE_RANGE = 150.0
VMEM_LIMIT = 52 * 1024 * 1024

GD_WIDTH = 1664

_NT = (((1,), (1,)), ((), ()))
_TN = (((0,), (0,)), ((), ()))


def _cparams(sem):
    return pltpu.CompilerParams(dimension_semantics=sem, vmem_limit_bytes=VMEM_LIMIT)


def _const_spec(shape):
    return pl.BlockSpec(shape, lambda *_: (0,) * len(shape), pipeline_mode=pl.Buffered(1))


def _rms(x, gain):
    return x * lax.rsqrt(jnp.mean(x * x, axis=-1, keepdims=True) + EPS) * gain


def _norm_mod(x, gain, sc, sh):
    return _rms(x, gain) * (1.0 + sc) + sh


def _silu(x):
    return x * jax.nn.sigmoid(x)


def _log_sigmoid(x):
    return jnp.minimum(x, 0.0) - jnp.log1p(jnp.exp(-jnp.abs(x)))


def _mod_kernel(c_ref, w_ref, b_ref, o_ref):
    a = _silu(c_ref[...]).astype(BF16)
    o_ref[0] = jnp.dot(a, w_ref[0].astype(BF16), preferred_element_type=F32) + b_ref[0]


def _modulation(c_rows, w_mod, b_mod):
    rows = c_rows.shape[0]
    return pl.pallas_call(
        _mod_kernel,
        grid=(DEPTH, 6),
        in_specs=[pl.BlockSpec((rows, D_MODEL), lambda l, j: (0, 0)),
                  pl.BlockSpec((1, D_MODEL, D_MODEL), lambda l, j: (l, 0, j)),
                  pl.BlockSpec((1, 1, D_MODEL), lambda l, j: (l, 0, j))],
        out_specs=pl.BlockSpec((1, rows, D_MODEL), lambda l, j: (l, 0, j)),
        out_shape=jax.ShapeDtypeStruct((DEPTH, rows, 6 * D_MODEL), F32),
        compiler_params=_cparams(("parallel", "parallel")),
        name="modulation",
    )(c_rows, w_mod, b_mod.reshape(DEPTH, 1, 6 * D_MODEL))


ROPE_SLABS = 5


CW = C_HEADS * C_DK
DKW = D_HEADS * D_DK
ZCB_WIDTH = 5 * CW
ZDB_WIDTH = 2 * DKW + 2 * BRANCH_W

W_IN_A = (0, A_HEADS * HEAD_DIM + 2 * A_KV_HEADS * HEAD_DIM)
W_IN_B = (W_IN_A[1], W_IN_A[1] + 3 * B_HEADS * HEAD_DIM)
W_IN_C = (W_IN_B[1], W_IN_B[1] + 5 * CW)
W_IN_D = (W_IN_C[1], W_IN_C[1] + GD_WIDTH)
W_IN_MERGE = (W_IN_C[1] + ZDB_WIDTH + 2 * D_GATE_RANK, W_IN_C[1] + ZDB_WIDTH + 2 * D_GATE_RANK + N_BRANCH * D_MODEL)


def _in_proj_kernel(x_ref, mod_ref, gain_ref, w_ref, cos_ref, s1_ref, s2_ref,
                    lb_ref, up_ref, gb_ref, za_ref, zb_ref, zcb_ref, zcg_ref, zdb_ref, zdg_ref):
    m = mod_ref[0]
    lb = lb_ref[...]
    wa_ref = w_ref.at[:, W_IN_A[0]:W_IN_A[1]]
    wb_ref = w_ref.at[:, W_IN_B[0]:W_IN_B[1]]
    wc_ref = w_ref.at[:, W_IN_C[0]:W_IN_C[1]]
    wd_ref = w_ref.at[:, W_IN_D[0]:W_IN_D[1]]
    gk0 = 2 * DKW + BRANCH_W
    dg0 = gk0 + 2 * D_GATE_RANK
    def project(r):
        rs = slice(r * SUB_ROWS, (r + 1) * SUB_ROWS)
        h = _norm_mod(x_ref[rs, :], gain_ref[0:1, :], m[1:2, :], m[0:1, :]).astype(BF16)
        return (rs,) + tuple(jnp.dot(h, w[...], preferred_element_type=F32)
                             for w in (wa_ref, wb_ref, wc_ref, wd_ref))

    for rs, z, z_b, z_c, z_d in [project(r) for r in range(TM // SUB_ROWS)]:
        cos, s1, s2 = cos_ref[rs, :], s1_ref[rs, :], s2_ref[rs, :]
        for s in range(ROPE_SLABS):
            zs = z[:, s * LANES:(s + 1) * LANES]
            rot = (zs * cos + pltpu.roll(zs, LANES - 16, axis=1) * s1
                   + pltpu.roll(zs, 16, axis=1) * s2)
            za_ref[rs, s * LANES:(s + 1) * LANES] = rot.astype(za_ref.dtype)
        za_ref[rs, ROPE_SLABS * LANES:] = z[:, ROPE_SLABS * LANES:].astype(za_ref.dtype)
        zb_ref[rs, :] = z_b.astype(zb_ref.dtype)
        z = z_c
        zcb_ref[rs, 0:CW] = _silu(z[:, 0:CW]).astype(zcb_ref.dtype)
        for d in range(2):
            forget = jax.nn.sigmoid(z[:, (1 + d) * CW:(2 + d) * CW])
            zcg_ref[rs, d * CW:(d + 1) * CW] = jnp.log(lb + (1.0 - lb) * forget)
            key = (1.0 - lb) * (1.0 - forget)
            zcb_ref[rs, (1 + d) * CW:(2 + d) * CW] = key.astype(zcb_ref.dtype)
        zcb_ref[rs, 3 * CW:] = z[:, 3 * CW:].astype(zcb_ref.dtype)
        z = z_d
        zdb_ref[rs, 0:DKW] = (z[:, 0:DKW] * (D_DK ** -0.5)).astype(zdb_ref.dtype)
        zdb_ref[rs, DKW:gk0] = z[:, DKW:gk0].astype(zdb_ref.dtype)
        zdb_ref[rs, gk0:] = z[:, dg0:dg0 + BRANCH_W].astype(zdb_ref.dtype)
        for d in range(2):
            zg = z[:, gk0 + d * D_GATE_RANK:gk0 + (d + 1) * D_GATE_RANK].astype(BF16)
            graw = jnp.dot(zg, up_ref[d].astype(BF16), preferred_element_type=F32) + gb_ref[d]
            zdg_ref[rs, d * DKW:(d + 1) * DKW] = _log_sigmoid(graw) / D_GATE_NORM


def _in_proj(xs, mod, gains, w, rope, lb, gate_up, gate_bias, bsz):
    rows = xs.shape[0]
    n_lat_tiles = bsz * SEQ // TM
    tiles_per_seq = SEQ // TM

    def mod_idx(i):
        return (jnp.where(i < n_lat_tiles, i // tiles_per_seq, bsz), 0, 0)

    def rope_idx(i):
        return (jnp.where(i < n_lat_tiles, i % tiles_per_seq, tiles_per_seq), 0)

    outs = ((W_IN_A[1] - W_IN_A[0], BF16), (W_IN_B[1] - W_IN_B[0], BF16), (ZCB_WIDTH, BF16), (2 * CW, F32),
            (ZDB_WIDTH, BF16), (2 * DKW, F32))
    return pl.pallas_call(
        _in_proj_kernel,
        grid=(rows // TM,),
        in_specs=([pl.BlockSpec((TM, D_MODEL), lambda i: (i, 0)),
                   pl.BlockSpec((1, 8, D_MODEL), mod_idx),
                   _const_spec((4, D_MODEL)),
                   _const_spec(w.shape)]
                  + [pl.BlockSpec((TM, LANES), rope_idx)] * 3
                  + [_const_spec(lb.shape), _const_spec(gate_up.shape), _const_spec(gate_bias.shape)]),
        out_specs=[pl.BlockSpec((TM, width), lambda i: (i, 0)) for width, _ in outs],
        out_shape=[jax.ShapeDtypeStruct((rows, width), dt) for width, dt in outs],
        compiler_params=_cparams(("parallel",)),
        name="in_proj",
    )(xs, mod, gains, w, *rope, lb, gate_up, gate_bias)


def _rope_tables():
    t = jnp.arange(SEQ)
    row = (t // GRID_W).astype(F32)
    col = (t % GRID_W).astype(F32)
    half = HEAD_DIM // 2
    inv = ROPE_BASE ** (-jnp.arange(0, half, 2, dtype=F32) / half)
    lane = np.arange(LANES)
    hl = lane % HEAD_DIM
    use_row = (hl // half) == 0
    freq = hl % (half // 2)
    first = (hl % half) < (half // 2)
    ang = jnp.where(use_row[None, :], row[:, None], col[:, None]) * inv[freq][None, :]
    cos, sin = jnp.cos(ang), jnp.sin(ang)
    s1 = jnp.where(first[None, :], -sin, 0.0)
    s2 = jnp.where(first[None, :], 0.0, sin)
    ident = jnp.ones((TM, LANES), F32)
    zero = jnp.zeros((TM, LANES), F32)
    return (jnp.concatenate([cos, ident]), jnp.concatenate([s1, zero]), jnp.concatenate([s2, zero]))


def _attn_a_kernel(sink_ref, q_ref, kp_ref, kc_ref, kn_ref, vp_ref, vc_ref, vn_ref, kx_ref, vx_ref, mask_ref,
                   o_ref):
    grp = A_HEADS // A_KV_HEADS
    rows = grp * A_BLOCK
    scale = HEAD_DIM ** -0.5
    nwin = 3 * A_BLOCK
    mask = jnp.concatenate([mask_ref[0]] * grp, axis=0)
    head_of_row = lax.broadcasted_iota(jnp.int32, (rows, 1), 0) >> 7

    def scores(g):
        hs = slice(g * HEAD_DIM, (g + 1) * HEAD_DIM)
        q = jnp.concatenate([q_ref[:, (g * grp + j) * HEAD_DIM:(g * grp + j + 1) * HEAD_DIM]
                             for j in range(grp)], axis=0) * scale
        k = jnp.concatenate([kp_ref[:, hs], kc_ref[:, hs], kn_ref[:, hs], kx_ref[:, hs]], axis=0)
        return lax.dot_general(q, k, _NT, preferred_element_type=F32)

    all_scores = [scores(g) for g in range(A_KV_HEADS)]
    ones = jnp.ones((3 * A_BLOCK + CTX_LEN, HEAD_DIM), BF16)
    for g in range(A_KV_HEADS):
        hs = slice(g * HEAD_DIM, (g + 1) * HEAD_DIM)
        v = jnp.concatenate([vp_ref[:, hs], vc_ref[:, hs], vn_ref[:, hs], vx_ref[:, hs]], axis=0)
        s = jnp.concatenate([all_scores[g][:, :nwin] + mask, all_scores[g][:, nwin:]], axis=1)
        sink = jnp.zeros((rows, 1), F32)
        for j in range(grp):
            sink = jnp.where(head_of_row == j, sink_ref[g * grp + j], sink)
        m = jnp.maximum(jnp.max(s, axis=-1, keepdims=True), sink)
        p = jnp.exp(s - m).astype(BF16)
        ov = jnp.dot(p, jnp.concatenate([v, ones], axis=1), preferred_element_type=F32)
        o = ov[:, :HEAD_DIM] / (ov[:, HEAD_DIM:] + jnp.exp(sink - m))
        o = jnp.concatenate([o[j * A_BLOCK:(j + 1) * A_BLOCK] for j in range(grp)], axis=1)
        o_ref[:, g * grp * HEAD_DIM:(g + 1) * grp * HEAD_DIM] = o.astype(o_ref.dtype)


def _window_masks():
    nb = SEQ // A_BLOCK
    qi = np.arange(A_BLOCK)[:, None]
    kj = np.arange(3 * A_BLOCK)[None, :]
    out = []
    for i in (0, 1, nb - 1):
        qpos = i * A_BLOCK + qi
        kpos = (i - 1) * A_BLOCK + kj
        ok = (np.abs(kpos - qpos) <= A_WINDOW) & (kpos >= 0) & (kpos < SEQ)
        out.append(np.where(ok, 0.0, NEG_INF).astype(np.float32))
    return jnp.asarray(np.stack(out))


def _attn_a(za, sink, bsz, out_rows):
    nb = SEQ // A_BLOCK
    qcb = A_HEADS * HEAD_DIM // LANES
    ctx_blk0 = bsz * SEQ // CTX_LEN

    def mask_idx(b, i):
        return (jnp.where(i == 0, 0, jnp.where(i == nb - 1, 2, 1)), 0, 0)

    def kv_spec(col, off):
        return pl.BlockSpec((A_BLOCK, LANES),
                            lambda b, i: (b * nb + jnp.clip(i + off, 0, nb - 1), col))

    return pl.pallas_call(
        _attn_a_kernel,
        grid=(bsz, nb),
        in_specs=[pl.BlockSpec(memory_space=pltpu.SMEM),
                  pl.BlockSpec((A_BLOCK, A_HEADS * HEAD_DIM), lambda b, i: (b * nb + i, 0)),
                  kv_spec(qcb, -1), kv_spec(qcb, 0), kv_spec(qcb, 1),
                  kv_spec(qcb + 1, -1), kv_spec(qcb + 1, 0), kv_spec(qcb + 1, 1),
                  pl.BlockSpec((CTX_LEN, LANES), lambda b, i: (ctx_blk0 + b, qcb)),
                  pl.BlockSpec((CTX_LEN, LANES), lambda b, i: (ctx_blk0 + b, qcb + 1)),
                  pl.BlockSpec((1, A_BLOCK, 3 * A_BLOCK), mask_idx)],
        out_specs=pl.BlockSpec((A_BLOCK, BRANCH_W), lambda b, i: (b * nb + i, 0)),
        out_shape=jax.ShapeDtypeStruct((out_rows, BRANCH_W), BF16),
        compiler_params=_cparams(("parallel", "parallel")),
        name="attn_window",
    )(sink, za, za, za, za, za, za, za, za, za, _window_masks())


NA_QROWS = 4
NA_KROWS = 12
NA_QTOK = NA_QROWS * GRID_W
NA_WIN = NA_KROWS * GRID_W
NA_KEYS = NA_WIN + CTX_LEN


def _attn_b_kernel(q_ref, k0_ref, k1_ref, k2_ref, v0_ref, v1_ref, v2_ref, kx_ref, vx_ref, bias_ref, o_ref):
    scale = HEAD_DIM ** -0.5

    def scores(h):
        hs = slice(h * HEAD_DIM, (h + 1) * HEAD_DIM)
        q = q_ref[:, hs] * scale
        k = jnp.concatenate([k0_ref[:, hs], k1_ref[:, hs], k2_ref[:, hs], kx_ref[:, hs]], axis=0)
        return lax.dot_general(q, k, _NT, preferred_element_type=F32)

    outs = []
    ones = jnp.ones((NA_KEYS, HEAD_DIM), BF16)
    s_next = scores(0)
    for h in range(B_HEADS):
        hs = slice(h * HEAD_DIM, (h + 1) * HEAD_DIM)
        s = jnp.concatenate([s_next[:, :NA_WIN] + bias_ref[0, h], s_next[:, NA_WIN:]], axis=1)
        v = jnp.concatenate([v0_ref[:, hs], v1_ref[:, hs], v2_ref[:, hs], vx_ref[:, hs]], axis=0)
        m = jnp.max(s, axis=-1, keepdims=True)
        p = jnp.exp(s - m).astype(BF16)
        if h + 1 < B_HEADS:
            s_next = scores(h + 1)
        ov = jnp.dot(p, jnp.concatenate([v, ones], axis=1), preferred_element_type=F32)
        outs.append(ov[:, :HEAD_DIM] / ov[:, HEAD_DIM:])
    o_ref[...] = jnp.concatenate(outs, axis=1).astype(o_ref.dtype)


def _bias_expand_kernel(rb_ref, onehot_ref, o_ref):
    o_ref[...] = jnp.dot(rb_ref[...], onehot_ref[...], preferred_element_type=F32,
                         precision=lax.Precision.HIGHEST)


def _na_bias_classes(rel_bias_all):
    rows = SEQ // GRID_W
    n_dr, n_dc = 2 * NA_KH - 1, 2 * NA_KW - 1
    qc = np.arange(GRID_W)[:, None]
    kc = np.arange(GRID_W)[None, :]
    cstart = np.clip(qc - NA_KW // 2, 0, GRID_W - NA_KW)
    col_ok = (kc >= cstart) & (kc < cstart + NA_KW)
    dc_idx = np.clip(kc - qc, -(NA_KW - 1), NA_KW - 1) + (NA_KW - 1)
    onehot = (np.arange(n_dc + 1)[:, None] == dc_idx.reshape(1, -1)).astype(np.float32)
    n_tab = DEPTH * B_HEADS * n_dr
    rb = jnp.pad(rel_bias_all.astype(F32).reshape(n_tab, n_dc), ((0, 0), (0, 1)))
    dense = pl.pallas_call(
        _bias_expand_kernel,
        grid=(DEPTH,),
        in_specs=[pl.BlockSpec((n_tab // DEPTH, n_dc + 1), lambda l: (l, 0)),
                  pl.BlockSpec((n_dc + 1, GRID_W * GRID_W), lambda l: (0, 0))],
        out_specs=pl.BlockSpec((n_tab // DEPTH, GRID_W * GRID_W), lambda l: (l, 0)),
        out_shape=jax.ShapeDtypeStruct((n_tab, GRID_W * GRID_W), F32),
        compiler_params=_cparams(("parallel",)),
        name="bias_expand",
    )(rb, jnp.asarray(onehot))
    dense = dense.reshape(DEPTH, B_HEADS, n_dr, GRID_W, GRID_W)
    masked = jnp.where(col_ok[None, None, None], dense, NEG_INF)
    n_groups = rows // NA_QROWS
    out = []
    for g in (0, 1, n_groups - 1):
        base = _na_key_base(g)
        per_row = []
        for lr in range(NA_QROWS):
            r = g * NA_QROWS + lr
            r0 = min(max(r - NA_KH // 2, 0), rows - NA_KH)
            dr_lo = r0 - r + (NA_KH - 1)
            before = r0 - base
            piece = jnp.pad(masked[:, :, dr_lo:dr_lo + NA_KH],
                            ((0, 0), (0, 0), (before, NA_KROWS - NA_KH - before), (0, 0), (0, 0)),
                            constant_values=NEG_INF)
            per_row.append(piece.transpose(0, 1, 3, 2, 4).reshape(DEPTH, B_HEADS, GRID_W, NA_KROWS * GRID_W))
        out.append(jnp.concatenate(per_row, axis=2))
    return jnp.stack(out, axis=1)


def _na_key_base(g):
    rows = SEQ // GRID_W
    return min(max(g * NA_QROWS - NA_KH // 2, 0), rows - NA_KROWS)


def _attn_b(zb, bias_cls, bsz, out_rows):
    n_groups = SEQ // NA_QTOK
    ctx_blk0 = bsz * SEQ // CTX_LEN
    max_base_blk = _na_key_base(n_groups - 1) // NA_QROWS

    def cls_idx(g, b):
        return (jnp.where(g == 0, 0, jnp.where(g == n_groups - 1, 2, 1)), 0, 0, 0)

    def kv_spec(col, j):
        return pl.BlockSpec((NA_QTOK, BRANCH_W),
                            lambda g, b: (b * n_groups + jnp.clip(g - 1, 0, max_base_blk) + j, col))

    return pl.pallas_call(
        _attn_b_kernel,
        grid=(n_groups, bsz),
        in_specs=[pl.BlockSpec((NA_QTOK, BRANCH_W), lambda g, b: (b * n_groups + g, 0)),
                  kv_spec(1, 0), kv_spec(1, 1), kv_spec(1, 2),
                  kv_spec(2, 0), kv_spec(2, 1), kv_spec(2, 2),
                  pl.BlockSpec((CTX_LEN, BRANCH_W), lambda g, b: (ctx_blk0 + b, 1)),
                  pl.BlockSpec((CTX_LEN, BRANCH_W), lambda g, b: (ctx_blk0 + b, 2)),
                  pl.BlockSpec((1, B_HEADS, NA_QTOK, NA_WIN), cls_idx)],
        out_specs=pl.BlockSpec((NA_QTOK, BRANCH_W), lambda g, b: (b * n_groups + g, 0)),
        out_shape=jax.ShapeDtypeStruct((out_rows, BRANCH_W), BF16),
        compiler_params=_cparams(("arbitrary", "arbitrary")),
        name="attn_neighbourhood",
    )(zb, zb, zb, zb, zb, zb, zb, zb, zb, bias_cls)


def _attn_ctx_kernel(sink_ref, q_ref, k_ref, v_ref, lat_ref, o_ref, *, n_heads, n_kv):
    del lat_ref
    grp = n_heads // n_kv
    rows = grp * CTX_LEN
    scale = HEAD_DIM ** -0.5
    head_of_row = lax.broadcasted_iota(jnp.int32, (rows, 1), 0) >> 8
    for g in range(n_kv):
        hs = slice(g * HEAD_DIM, (g + 1) * HEAD_DIM)
        q = jnp.concatenate([q_ref[:, (g * grp + j) * HEAD_DIM:(g * grp + j + 1) * HEAD_DIM]
                             for j in range(grp)], axis=0)
        s = lax.dot_general(q, k_ref[:, hs], _NT, preferred_element_type=F32) * scale
        sink = jnp.zeros((rows, 1), F32)
        for j in range(grp):
            sink = jnp.where(head_of_row == j, sink_ref[g * grp + j], sink)
        m = jnp.maximum(jnp.max(s, axis=-1, keepdims=True), sink)
        p = jnp.exp(s - m)
        den = jnp.sum(p, axis=-1, keepdims=True) + jnp.exp(sink - m)
        o = jnp.dot(p.astype(BF16), v_ref[:, hs], preferred_element_type=F32) / den
        o = jnp.concatenate([o[j * CTX_LEN:(j + 1) * CTX_LEN] for j in range(grp)], axis=1)
        o_ref[:, g * grp * HEAD_DIM:(g + 1) * grp * HEAD_DIM] = o.astype(o_ref.dtype)


def _attn_ctx(z, sink, y_lat, bsz, n_heads, n_kv):
    ctx_blk0 = bsz * SEQ // CTX_LEN
    kvw = n_kv * HEAD_DIM
    kcol = n_heads * HEAD_DIM // kvw
    return pl.pallas_call(
        functools.partial(_attn_ctx_kernel, n_heads=n_heads, n_kv=n_kv),
        grid=(bsz,),
        in_specs=[pl.BlockSpec(memory_space=pltpu.SMEM),
                  pl.BlockSpec((CTX_LEN, n_heads * HEAD_DIM), lambda b: (ctx_blk0 + b, 0)),
                  pl.BlockSpec((CTX_LEN, kvw), lambda b: (ctx_blk0 + b, kcol)),
                  pl.BlockSpec((CTX_LEN, kvw), lambda b: (ctx_blk0 + b, kcol + 1)),
                  pl.BlockSpec(memory_space=pl.ANY)],
        out_specs=pl.BlockSpec((CTX_LEN, BRANCH_W), lambda b: (ctx_blk0 + b, 0)),
        out_shape=jax.ShapeDtypeStruct(y_lat.shape, y_lat.dtype),
        input_output_aliases={4: 0},
        compiler_params=_cparams(("parallel",)),
        name="attn_ctx_%d" % n_kv,
    )(sink, z, z, z, y_lat)


def _cumulative_decay(g, tri):
    trib = tri.astype(BF16)
    g_hi = g.astype(BF16)
    rem = g - g_hi.astype(F32)
    g_mid = rem.astype(BF16)
    g_lo = (rem - g_mid.astype(F32)).astype(BF16)
    return (jnp.dot(trib, g_hi, preferred_element_type=F32) + jnp.dot(trib, g_mid, preferred_element_type=F32)
            + jnp.dot(trib, g_lo, preferred_element_type=F32))


def _scan_chunks(chains, cums, tris):
    def scaled_operands(chain, cum):
        q, k, v, state, d, n_heads, dk = chain
        t = q.shape[0]
        tot = cum[t - 1:t] if d == 0 else cum[0:1]
        mid = 0.5 * tot
        e_mid = jnp.exp(mid)
        qa = q * jnp.exp(cum - mid)
        kb = k * jnp.exp(mid - cum)
        qd = (qa * e_mid).astype(BF16)
        kd = (kb * e_mid).astype(BF16)
        return (qa.astype(BF16), kb.astype(BF16), qd, kd, v.astype(BF16),
                state * jnp.exp(tot), state.astype(BF16))

    def unmasked_matmuls(chain, ops):
        q, k, v, state, d, n_heads, dk = chain
        qa, kb, qd, kd, vb, decayed, state_b = ops
        per_head = []
        for h in range(n_heads):
            ks = slice(h * dk, (h + 1) * dk)
            vs = slice(h * LANES, (h + 1) * LANES)
            a = lax.dot_general(qa[:, ks], kb[:, ks], _NT, preferred_element_type=F32)
            carry = lax.dot_general(qd[:, ks], state_b[:, ks], _NT, preferred_element_type=F32)
            upd = lax.dot_general(vb[:, vs], kd[:, ks], _TN, preferred_element_type=F32)
            per_head.append((a, carry, decayed[:, ks] + upd))
        return per_head

    def masked_matmuls(chain, ops, per_head):
        d, vb = chain[4], ops[4]
        outs = []
        for h, (a, carry, new_s) in enumerate(per_head):
            a = jnp.where(tris[d], a, 0.0).astype(BF16)
            outs.append(jnp.dot(a, vb[:, h * LANES:(h + 1) * LANES], preferred_element_type=F32) + carry)
        return jnp.concatenate(outs, axis=1), jnp.concatenate([p[2] for p in per_head], axis=1)

    n = len(chains)
    ops, partial, results = [None] * n, [None] * n, [None] * n
    for i in range(n + 1):
        if i < n:
            ops[i] = scaled_operands(chains[i], cums[i])
            partial[i] = unmasked_matmuls(chains[i], ops[i])
        if i > 0:
            results[i - 1] = masked_matmuls(chains[i - 1], ops[i - 1], partial[i - 1])
    return results


def _scan_tokens(ref_chains):
    n_groups = SCAN_T // SCAN_GROUP
    rows = lax.broadcasted_iota(jnp.int32, (SCAN_GROUP, 1), 0)

    def group_body(gi, carry):
        for (q_ref, q_off, k_ref, k_off, v_ref, g_ref, o_ref, s_ref, d, n_heads, dk) in ref_chains:
            w = n_heads * dk
            g_idx = gi if d == 0 else n_groups - 1 - gi
            base = pl.multiple_of(g_idx * SCAN_GROUP, SCAN_GROUP)
            q16 = q_ref[pl.ds(base, SCAN_GROUP), q_off:q_off + w].astype(F32)
            k16 = k_ref[pl.ds(base, SCAN_GROUP), k_off:k_off + w].astype(F32)
            v16 = v_ref[pl.ds(base, SCAN_GROUP), :].astype(F32)
            g16 = g_ref[pl.ds(base, SCAN_GROUP), :]

            def token_body(jj, out16, q16=q16, k16=k16, v16=v16, g16=g16, s_ref=s_ref, d=d, n_heads=n_heads,
                           dk=dk):
                j = jj if d == 0 else SCAN_GROUP - 1 - jj
                sel = rows == j
                decay = jnp.exp(jnp.sum(jnp.where(sel, g16, 0.0), axis=0, keepdims=True))
                s = s_ref[...] * decay
                kj = jnp.where(sel, k16, 0.0)
                qj = jnp.where(sel, q16, 0.0)
                outs, cols = [], []
                for h in range(n_heads):
                    ks = slice(h * dk, (h + 1) * dk)
                    vs = slice(h * LANES, (h + 1) * LANES)
                    s_h = s[:, ks] + lax.dot_general(v16[:, vs], kj[:, ks], _TN, preferred_element_type=F32)
                    cols.append(s_h)
                    outs.append(lax.dot_general(qj[:, ks], s_h, _NT, preferred_element_type=F32))
                s_ref[...] = jnp.concatenate(cols, axis=1)
                return out16 + jnp.concatenate(outs, axis=1)

            out16 = lax.fori_loop(0, SCAN_GROUP, token_body, jnp.zeros((SCAN_GROUP, BRANCH_W), F32))
            o_ref[pl.ds(base, SCAN_GROUP), :] = out16
        return carry

    lax.fori_loop(0, n_groups, group_body, 0)


def _scan_kernel(cq_f, ck_f, ci_f, cg_f, cq_b, ck_b, ci_b, cg_b, dqk_f, dv_f, dg_f, dqk_b, dv_b, dg_b,
                 ocf_ref, ocb_ref, odf_ref, odb_ref, scf_scr, scb_scr, sdf_scr, sdb_scr):
    scratches = (scf_scr, scb_scr, sdf_scr, sdb_scr)

    @pl.when(pl.program_id(1) == 0)
    def _():
        for s in scratches:
            s[...] = jnp.zeros_like(s)

    dirs = ((cq_f, ck_f, ci_f, cg_f, dqk_f, dv_f, dg_f), (cq_b, ck_b, ci_b, cg_b, dqk_b, dv_b, dg_b))
    total = jnp.zeros((1, LANES), F32)
    for g_ref in (cg_f, cg_b, dg_f, dg_b):
        col_tot = jnp.sum(g_ref[...], axis=0, keepdims=True)
        for j in range(col_tot.shape[1] // LANES):
            total = jnp.minimum(total, col_tot[:, j * LANES:(j + 1) * LANES])
    in_range = jnp.min(total) > -SCAN_SAFE_RANGE

    t = SCAN_T
    row = lax.broadcasted_iota(jnp.int32, (t, t), 0)
    col = lax.broadcasted_iota(jnp.int32, (t, t), 1)
    tris = (col <= row, col >= row)
    cums = []
    for d, (cq, ck, ci, cg, dqk, dv, dg) in enumerate(dirs):
        cums += [_cumulative_decay(cg[...], tris[d]), _cumulative_decay(dg[...], tris[d])]

    @pl.when(in_range)
    def _():
        states = [s[...] for s in scratches]
        chains = []
        for d, (cq, ck, ci, cg, dqk, dv, dg) in enumerate(dirs):
            chains.append((cq[...].astype(F32), ck[...].astype(F32), ci[...], states[d], d, C_HEADS, C_DK))
            qk = dqk[...].astype(F32)
            chains.append((qk[:, :DKW], qk[:, DKW:], dv[...], states[2 + d], d, D_HEADS, D_DK))
        (ocf, scf), (odf, sdf), (ocb, scb), (odb, sdb) = _scan_chunks(chains, cums, tris)
        for ref, val in ((ocf_ref, ocf), (ocb_ref, ocb), (odf_ref, odf), (odb_ref, odb),
                         (scf_scr, scf), (scb_scr, scb), (sdf_scr, sdf), (sdb_scr, sdb)):
            ref[...] = val

    @pl.when(jnp.logical_not(in_range))
    def _():
        ref_chains = []
        for d, (cq, ck, ci, cg, dqk, dv, dg) in enumerate(dirs):
            ref_chains.append((cq, 0, ck, 0, ci, cg, (ocf_ref, ocb_ref)[d], scratches[d], d, C_HEADS, C_DK))
            ref_chains.append((dqk, 0, dqk, DKW, dv, dg, (odf_ref, odb_ref)[d], scratches[2 + d], d,
                               D_HEADS, D_DK))
        _scan_tokens(ref_chains)


def _scan(zcb, zcg, zdb, zdg, bsz):
    t = SCAN_T
    n_ctx = CTX_LEN // t
    n_lat = SEQ // t
    ctx_blk0 = bsz * n_lat

    def fwd(b, c):
        return jnp.where(c < n_ctx, ctx_blk0 + b * n_ctx + c, b * n_lat + c - n_ctx)

    def bwd(b, c):
        return jnp.where(c < n_ctx, ctx_blk0 + b * n_ctx + (n_ctx - 1 - c), b * n_lat + (n_lat + n_ctx - 1 - c))

    def spec(order, width, col):
        return pl.BlockSpec((t, width), lambda b, c: (order(b, c), col))

    in_specs = [spec(fwd, CW, 0), spec(fwd, CW, 1), spec(fwd, CW, 3), spec(fwd, CW, 0),
                spec(bwd, CW, 0), spec(bwd, CW, 2), spec(bwd, CW, 3), spec(bwd, CW, 1),
                spec(fwd, 2 * DKW, 0), spec(fwd, BRANCH_W, 1), spec(fwd, DKW, 0),
                spec(bwd, 2 * DKW, 0), spec(bwd, BRANCH_W, 1), spec(bwd, DKW, 1)]
    rows = zcb.shape[0]
    out = jax.ShapeDtypeStruct((rows, BRANCH_W), F32)
    return pl.pallas_call(
        _scan_kernel,
        grid=(bsz, n_ctx + n_lat),
        in_specs=in_specs,
        out_specs=[spec(fwd, 512, 0), spec(bwd, 512, 0), spec(fwd, 512, 0), spec(bwd, 512, 0)],
        out_shape=[out, out, out, out],
        scratch_shapes=[pltpu.VMEM((LANES, C_HEADS * C_DK), F32), pltpu.VMEM((LANES, C_HEADS * C_DK), F32),
                        pltpu.VMEM((LANES, D_HEADS * D_DK), F32), pltpu.VMEM((LANES, D_HEADS * D_DK), F32)],
        compiler_params=_cparams(("parallel", "arbitrary")),
        name="bidir_scan",
    )(zcb, zcb, zcb, zcg, zcb, zcb, zcb, zcg, zdb, zdb, zdg, zdb, zdb, zdg)


def _post_kernel(x_ref, mod_ref, gain_ref, ya_ref, yb_ref, ocf_ref, ocb_ref, odf_ref, odb_ref,
                 cg_ref, dg_ref, cn_ref, dn_ref, wm_ref, wb_ref, wo_ref, o_ref):
    m = mod_ref[0]

    def gated_group_norm(o, gain, gate):
        parts = [_rms(o[:, j * LANES:(j + 1) * LANES], gain) for j in range(BRANCH_W // LANES)]
        return (jnp.concatenate(parts, axis=1) * _silu(gate.astype(F32))).astype(BF16)

    for r in range(TM_POST // SUB_ROWS):
        rs = slice(r * SUB_ROWS, (r + 1) * SUB_ROWS)
        x = x_ref[rs, :]
        h = _norm_mod(x, gain_ref[0:1, :], m[1:2, :], m[0:1, :]).astype(BF16)
        y_c = gated_group_norm(ocf_ref[rs, :] + ocb_ref[rs, :], cn_ref[...], cg_ref[rs, :])
        y_d = gated_group_norm(odf_ref[rs, :] + odb_ref[rs, :], dn_ref[...], dg_ref[rs, :])
        branches = (ya_ref[rs, :], yb_ref[rs, :], y_c, y_d)
        acc = None
        for j, yb in enumerate(branches):
            gate = jnp.dot(h, wm_ref[:, j * D_MODEL:(j + 1) * D_MODEL], preferred_element_type=F32)
            term = jax.nn.sigmoid(gate) * jnp.dot(yb, wb_ref[j], preferred_element_type=F32)
            acc = term if acc is None else acc + term
        y = jnp.dot(acc.astype(BF16), wo_ref[...], preferred_element_type=F32)
        o_ref[rs, :] = x + m[2:3, :] * _rms(y, gain_ref[1:2, :])


def _post(xs, mod, gains, ya, yb, ocf, ocb, odf, odb, zc, zd, c_norm, d_norm, wm, wb, wo, bsz, n_rows):
    tm = TM_POST
    n_lat_tiles = bsz * SEQ // tm
    tiles_per_seq = SEQ // tm

    def mod_idx(i):
        return (jnp.where(i < n_lat_tiles, i // tiles_per_seq, bsz), 0, 0)

    row = lambda w, col=0: pl.BlockSpec((tm, w), lambda i: (i, col))
    const = _const_spec
    return pl.pallas_call(
        _post_kernel,
        grid=(n_rows // tm,),
        in_specs=[row(D_MODEL), pl.BlockSpec((1, 8, D_MODEL), mod_idx), const((4, D_MODEL)),
                  row(BRANCH_W), row(BRANCH_W), row(BRANCH_W), row(BRANCH_W), row(BRANCH_W), row(BRANCH_W),
                  row(BRANCH_W, 4), row(BRANCH_W, 2),
                  const((1, LANES)), const((1, LANES)),
                  const((D_MODEL, N_BRANCH * D_MODEL)), const((N_BRANCH, BRANCH_W, D_MODEL)),
                  const((D_MODEL, D_MODEL))],
        out_specs=row(D_MODEL),
        out_shape=jax.ShapeDtypeStruct((n_rows, D_MODEL), F32),
        compiler_params=_cparams(("parallel",)),
        name="merge_out",
    )(xs, mod, gains, ya, yb, ocf, ocb, odf, odb, zc, zd, c_norm, d_norm, wm, wb, wo)


def _ffn_kernel(x_ref, mod_ref, gain_ref, w1_ref, w2_ref, o_ref):
    m = mod_ref[0]
    for r in range(TM // SUB_ROWS):
        rs = slice(r * SUB_ROWS, (r + 1) * SUB_ROWS)
        x = x_ref[rs, :]
        h = _norm_mod(x, gain_ref[2:3, :], m[4:5, :], m[3:4, :]).astype(BF16)
        acc = None
        for k in range(D_FF // FF_CHUNK):
            ks = slice(k * FF_CHUNK, (k + 1) * FF_CHUNK)
            u = jnp.dot(h, w1_ref[:, ks], preferred_element_type=F32)
            u = jnp.square(jnp.maximum(u, 0.0)).astype(BF16)
            part = jnp.dot(u, w2_ref[ks, :], preferred_element_type=F32)
            acc = part if acc is None else acc + part
        o_ref[rs, :] = x + m[5:6, :] * _rms(acc, gain_ref[3:4, :])


def _ffn(xs, mod, gains, w1, w2, bsz, n_rows):
    n_lat_tiles = bsz * SEQ // TM
    tiles_per_seq = SEQ // TM

    def mod_idx(i):
        return (jnp.where(i < n_lat_tiles, i // tiles_per_seq, bsz), 0, 0)

    return pl.pallas_call(
        _ffn_kernel,
        grid=(n_rows // TM,),
        in_specs=[pl.BlockSpec((TM, D_MODEL), lambda i: (i, 0)),
                  pl.BlockSpec((1, 8, D_MODEL), mod_idx),
                  _const_spec((4, D_MODEL)),
                  _const_spec((D_MODEL, D_FF)),
                  _const_spec((D_FF, D_MODEL))],
        out_specs=pl.BlockSpec((TM, D_MODEL), lambda i: (i, 0)),
        out_shape=jax.ShapeDtypeStruct((n_rows, D_MODEL), F32),
        compiler_params=_cparams(("parallel",)),
        name="mlp",
    )(xs, mod, gains, w1, w2)


def kernel(x, c, ctx, c_ctx, w_mod, b_mod, norm_gains, w_in, a_sink, b_rel_bias, c_lower_bounds, c_norm,
           d_gate_up, d_gate_bias, d_norm, w_branch, w_out, w_ff1, w_ff2):
    bsz = x.shape[0]
    n_lat = bsz * SEQ
    n_rows = n_lat + bsz * CTX_LEN

    lb_soft = jax.nn.softmax(c_lower_bounds.astype(F32), axis=0)
    lb_all = jnp.cumsum(lb_soft, axis=0) - lb_soft[0:1]

    c_rows = jnp.concatenate([c, c_ctx[None, :], jnp.zeros((-(bsz + 1) % 8, D_MODEL), F32)], axis=0)
    mod_all = _modulation(c_rows, w_mod, b_mod)[:, :bsz + 1]
    mod_all = mod_all.reshape(DEPTH, bsz + 1, 6, D_MODEL)
    mod_all = jnp.pad(mod_all, ((0, 0), (0, 0), (0, 2), (0, 0)))

    rope = _rope_tables()
    bias_cls = _na_bias_classes(b_rel_bias)
    no_sink = jnp.full((B_HEADS,), NEG_INF, F32)
    xs = jnp.concatenate([x.reshape(n_lat, D_MODEL), ctx.reshape(bsz * CTX_LEN, D_MODEL)], axis=0)

    for l in range(DEPTH):
        need_ctx = l < DEPTH - 1
        mod, gains = mod_all[l], norm_gains[l]
        w_proj = w_in[l, :, :W_IN_D[1]].astype(BF16)
        w_m = w_in[l, :, W_IN_MERGE[0]:W_IN_MERGE[1]].astype(BF16)
        za, zb, zcb, zcg, zdb, zdg = _in_proj(xs, mod, gains, w_proj, rope,
                                              lb_all[l].reshape(1, -1), d_gate_up[l],
                                              d_gate_bias[l].reshape(2, 1, -1), bsz)

        rows_l = n_rows if need_ctx else n_lat
        ya = _attn_a(za, a_sink[l], bsz, rows_l)
        yb = _attn_b(zb, bias_cls[l], bsz, rows_l)
        if need_ctx:
            ya = _attn_ctx(za, a_sink[l], ya, bsz, A_HEADS, A_KV_HEADS)
            yb = _attn_ctx(zb, no_sink, yb, bsz, B_HEADS, B_HEADS)
        ocf, ocb, odf, odb = _scan(zcb, zcg, zdb, zdg, bsz)

        x_mid = _post(xs, mod, gains, ya, yb, ocf, ocb, odf, odb, zcb, zdb,
                      c_norm[l].reshape(1, -1), d_norm[l].reshape(1, -1),
                      w_m, w_branch[l].astype(BF16), w_out[l].astype(BF16), bsz, rows_l)
        xs = _ffn(x_mid, mod, gains, w_ff1[l].astype(BF16), w_ff2[l].astype(BF16), bsz, rows_l)
    return xs[:n_lat].reshape(bsz, SEQ, D_MODEL)
```

```python
import functools

import numpy as np
import jax
import jax.numpy as jnp
from jax import lax
from jax.experimental import pallas as pl
from jax.experimental.pallas import tpu as pltpu

D_MODEL = 1024
SEQ = 2048
DEPTH = 4
CTX_LEN = 256
GRID_W = 64
HEAD_DIM = 64
A_HEADS = 8
A_KV_HEADS = 2
A_WINDOW = 128
A_BLOCK = 128
B_HEADS = 8
NA_KH = 8
NA_KW = 16
C_HEADS = 4
C_DK = 128
D_HEADS = 4
D_DK = 64
D_GATE_RANK = 16
D_GATE_NORM = 16.0
N_BRANCH = 4
D_FF = 4 * D_MODEL
ROPE_BASE = 10000.0
EPS = 1e-6
NEG_INF = -1e30
BRANCH_W = 512

F32 = jnp.float32
BF16 = jnp.bfloat16

LANES = 128
TM = 512
TM_POST = 512
SUB_ROWS = 256
FF_CHUNK = 1024
SCAN_T = 64
SCAN_GROUP = 16
SCAN_SAFE_RANGE = 150.0
VMEM_LIMIT = 52 * 1024 * 1024

GD_WIDTH = 1664

_NT = (((1,), (1,)), ((), ()))
_TN = (((0,), (0,)), ((), ()))


def _cparams(sem):
    return pltpu.CompilerParams(dimension_semantics=sem, vmem_limit_bytes=VMEM_LIMIT)


def _const_spec(shape):
    return pl.BlockSpec(shape, lambda *_: (0,) * len(shape), pipeline_mode=pl.Buffered(1))


def _rms(x, gain):
    return x * lax.rsqrt(jnp.mean(x * x, axis=-1, keepdims=True) + EPS) * gain


def _norm_mod(x, gain, sc, sh):
    return _rms(x, gain) * (1.0 + sc) + sh


def _silu(x):
    return x * jax.nn.sigmoid(x)


def _log_sigmoid(x):
    return jnp.minimum(x, 0.0) - jnp.log1p(jnp.exp(-jnp.abs(x)))


def _mod_kernel(c_ref, w_ref, b_ref, o_ref):
    a = _silu(c_ref[...]).astype(BF16)
    o_ref[0] = jnp.dot(a, w_ref[0].astype(BF16), preferred_element_type=F32) + b_ref[0]


def _modulation(c_rows, w_mod, b_mod):
    rows = c_rows.shape[0]
    return pl.pallas_call(
        _mod_kernel,
        grid=(DEPTH, 6),
        in_specs=[pl.BlockSpec((rows, D_MODEL), lambda l, j: (0, 0)),
                  pl.BlockSpec((1, D_MODEL, D_MODEL), lambda l, j: (l, 0, j)),
                  pl.BlockSpec((1, 1, D_MODEL), lambda l, j: (l, 0, j))],
        out_specs=pl.BlockSpec((1, rows, D_MODEL), lambda l, j: (l, 0, j)),
        out_shape=jax.ShapeDtypeStruct((DEPTH, rows, 6 * D_MODEL), F32),
        compiler_params=_cparams(("parallel", "parallel")),
        name="modulation",
    )(c_rows, w_mod, b_mod.reshape(DEPTH, 1, 6 * D_MODEL))


ROPE_SLABS = 5


CW = C_HEADS * C_DK
DKW = D_HEADS * D_DK
ZCB_WIDTH = 5 * CW
ZDB_WIDTH = 2 * DKW + 2 * BRANCH_W

W_IN_A = (0, A_HEADS * HEAD_DIM + 2 * A_KV_HEADS * HEAD_DIM)
W_IN_B = (W_IN_A[1], W_IN_A[1] + 3 * B_HEADS * HEAD_DIM)
W_IN_C = (W_IN_B[1], W_IN_B[1] + 5 * CW)
W_IN_D = (W_IN_C[1], W_IN_C[1] + GD_WIDTH)
W_IN_MERGE = (W_IN_C[1] + ZDB_WIDTH + 2 * D_GATE_RANK, W_IN_C[1] + ZDB_WIDTH + 2 * D_GATE_RANK + N_BRANCH * D_MODEL)


def _in_proj_kernel(x_ref, mod_ref, gain_ref, w_ref, cos_ref, s1_ref, s2_ref,
                    lb_ref, up_ref, gb_ref, za_ref, zb_ref, zcb_ref, zcg_ref, zdb_ref, zdg_ref):
    m = mod_ref[0]
    lb = lb_ref[...]
    wa_ref = w_ref.at[:, W_IN_A[0]:W_IN_A[1]]
    wb_ref = w_ref.at[:, W_IN_B[0]:W_IN_B[1]]
    wc_ref = w_ref.at[:, W_IN_C[0]:W_IN_C[1]]
    wd_ref = w_ref.at[:, W_IN_D[0]:W_IN_D[1]]
    gk0 = 2 * DKW + BRANCH_W
    dg0 = gk0 + 2 * D_GATE_RANK
    def project(r):
        rs = slice(r * SUB_ROWS, (r + 1) * SUB_ROWS)
        h = _norm_mod(x_ref[rs, :], gain_ref[0:1, :], m[1:2, :], m[0:1, :]).astype(BF16)
        return (rs,) + tuple(jnp.dot(h, w[...], preferred_element_type=F32)
                             for w in (wa_ref, wb_ref, wc_ref, wd_ref))

    for rs, z, z_b, z_c, z_d in [project(r) for r in range(TM // SUB_ROWS)]:
        cos, s1, s2 = cos_ref[rs, :], s1_ref[rs, :], s2_ref[rs, :]
        for s in range(ROPE_SLABS):
            zs = z[:, s * LANES:(s + 1) * LANES]
            rot = (zs * cos + pltpu.roll(zs, LANES - 16, axis=1) * s1
                   + pltpu.roll(zs, 16, axis=1) * s2)
            za_ref[rs, s * LANES:(s + 1) * LANES] = rot.astype(za_ref.dtype)
        za_ref[rs, ROPE_SLABS * LANES:] = z[:, ROPE_SLABS * LANES:].astype(za_ref.dtype)
        zb_ref[rs, :] = z_b.astype(zb_ref.dtype)
        z = z_c
        zcb_ref[rs, 0:CW] = _silu(z[:, 0:CW]).astype(zcb_ref.dtype)
        for d in range(2):
            forget = jax.nn.sigmoid(z[:, (1 + d) * CW:(2 + d) * CW])
            zcg_ref[rs, d * CW:(d + 1) * CW] = jnp.log(lb + (1.0 - lb) * forget)
            key = (1.0 - lb) * (1.0 - forget)
            zcb_ref[rs, (1 + d) * CW:(2 + d) * CW] = key.astype(zcb_ref.dtype)
        zcb_ref[rs, 3 * CW:] = z[:, 3 * CW:].astype(zcb_ref.dtype)
        z = z_d
        zdb_ref[rs, 0:DKW] = (z[:, 0:DKW] * (D_DK ** -0.5)).astype(zdb_ref.dtype)
        zdb_ref[rs, DKW:gk0] = z[:, DKW:gk0].astype(zdb_ref.dtype)
        zdb_ref[rs, gk0:] = z[:, dg0:dg0 + BRANCH_W].astype(zdb_ref.dtype)
        for d in range(2):
            zg = z[:, gk0 + d * D_GATE_RANK:gk0 + (d + 1) * D_GATE_RANK].astype(BF16)
            graw = jnp.dot(zg, up_ref[d].astype(BF16), preferred_element_type=F32) + gb_ref[d]
            zdg_ref[rs, d * DKW:(d + 1) * DKW] = _log_sigmoid(graw) / D_GATE_NORM


def _in_proj(xs, mod, gains, w, rope, lb, gate_up, gate_bias, bsz):
    rows = xs.shape[0]
    n_lat_tiles = bsz * SEQ // TM
    tiles_per_seq = SEQ // TM

    def mod_idx(i):
        return (jnp.where(i < n_lat_tiles, i // tiles_per_seq, bsz), 0, 0)

    def rope_idx(i):
        return (jnp.where(i < n_lat_tiles, i % tiles_per_seq, tiles_per_seq), 0)

    outs = ((W_IN_A[1] - W_IN_A[0], BF16), (W_IN_B[1] - W_IN_B[0], BF16), (ZCB_WIDTH, BF16), (2 * CW, F32),
            (ZDB_WIDTH, BF16), (2 * DKW, F32))
    return pl.pallas_call(
        _in_proj_kernel,
        grid=(rows // TM,),
        in_specs=([pl.BlockSpec((TM, D_MODEL), lambda i: (i, 0)),
                   pl.BlockSpec((1, 8, D_MODEL), mod_idx),
                   _const_spec((4, D_MODEL)),
                   _const_spec(w.shape)]
                  + [pl.BlockSpec((TM, LANES), rope_idx)] * 3
                  + [_const_spec(lb.shape), _const_spec(gate_up.shape), _const_spec(gate_bias.shape)]),
        out_specs=[pl.BlockSpec((TM, width), lambda i: (i, 0)) for width, _ in outs],
        out_shape=[jax.ShapeDtypeStruct((rows, width), dt) for width, dt in outs],
        compiler_params=_cparams(("parallel",)),
        name="in_proj",
    )(xs, mod, gains, w, *rope, lb, gate_up, gate_bias)


def _rope_tables():
    t = jnp.arange(SEQ)
    row = (t // GRID_W).astype(F32)
    col = (t % GRID_W).astype(F32)
    half = HEAD_DIM // 2
    inv = ROPE_BASE ** (-jnp.arange(0, half, 2, dtype=F32) / half)
    lane = np.arange(LANES)
    hl = lane % HEAD_DIM
    use_row = (hl // half) == 0
    freq = hl % (half // 2)
    first = (hl % half) < (half // 2)
    ang = jnp.where(use_row[None, :], row[:, None], col[:, None]) * inv[freq][None, :]
    cos, sin = jnp.cos(ang), jnp.sin(ang)
    s1 = jnp.where(first[None, :], -sin, 0.0)
    s2 = jnp.where(first[None, :], 0.0, sin)
    ident = jnp.ones((TM, LANES), F32)
    zero = jnp.zeros((TM, LANES), F32)
    return (jnp.concatenate([cos, ident]), jnp.concatenate([s1, zero]), jnp.concatenate([s2, zero]))


def _attn_a_kernel(sink_ref, q_ref, kp_ref, kc_ref, kn_ref, vp_ref, vc_ref, vn_ref, kx_ref, vx_ref, mask_ref,
                   o_ref):
    grp = A_HEADS // A_KV_HEADS
    rows = grp * A_BLOCK
    scale = HEAD_DIM ** -0.5
    nwin = 3 * A_BLOCK
    mask = jnp.concatenate([mask_ref[0]] * grp, axis=0)
    head_of_row = lax.broadcasted_iota(jnp.int32, (rows, 1), 0) >> 7

    def scores(g):
        hs = slice(g * HEAD_DIM, (g + 1) * HEAD_DIM)
        q = jnp.concatenate([q_ref[:, (g * grp + j) * HEAD_DIM:(g * grp + j + 1) * HEAD_DIM]
                             for j in range(grp)], axis=0) * scale
        k = jnp.concatenate([kp_ref[:, hs], kc_ref[:, hs], kn_ref[:, hs], kx_ref[:, hs]], axis=0)
        return lax.dot_general(q, k, _NT, preferred_element_type=F32)

    all_scores = [scores(g) for g in range(A_KV_HEADS)]
    ones = jnp.ones((3 * A_BLOCK + CTX_LEN, HEAD_DIM), BF16)
    for g in range(A_KV_HEADS):
        hs = slice(g * HEAD_DIM, (g + 1) * HEAD_DIM)
        v = jnp.concatenate([vp_ref[:, hs], vc_ref[:, hs], vn_ref[:, hs], vx_ref[:, hs]], axis=0)
        s = jnp.concatenate([all_scores[g][:, :nwin] + mask, all_scores[g][:, nwin:]], axis=1)
        sink = jnp.zeros((rows, 1), F32)
        for j in range(grp):
            sink = jnp.where(head_of_row == j, sink_ref[g * grp + j], sink)
        m = jnp.maximum(jnp.max(s, axis=-1, keepdims=True), sink)
        p = jnp.exp(s - m).astype(BF16)
        ov = jnp.dot(p, jnp.concatenate([v, ones], axis=1), preferred_element_type=F32)
        o = ov[:, :HEAD_DIM] / (ov[:, HEAD_DIM:] + jnp.exp(sink - m))
        o = jnp.concatenate([o[j * A_BLOCK:(j + 1) * A_BLOCK] for j in range(grp)], axis=1)
        o_ref[:, g * grp * HEAD_DIM:(g + 1) * grp * HEAD_DIM] = o.astype(o_ref.dtype)


def _window_masks():
    nb = SEQ // A_BLOCK
    qi = np.arange(A_BLOCK)[:, None]
    kj = np.arange(3 * A_BLOCK)[None, :]
    out = []
    for i in (0, 1, nb - 1):
        qpos = i * A_BLOCK + qi
        kpos = (i - 1) * A_BLOCK + kj
        ok = (np.abs(kpos - qpos) <= A_WINDOW) & (kpos >= 0) & (kpos < SEQ)
        out.append(np.where(ok, 0.0, NEG_INF).astype(np.float32))
    return jnp.asarray(np.stack(out))


def _attn_a(za, sink, bsz, out_rows):
    nb = SEQ // A_BLOCK
    qcb = A_HEADS * HEAD_DIM // LANES
    ctx_blk0 = bsz * SEQ // CTX_LEN

    def mask_idx(b, i):
        return (jnp.where(i == 0, 0, jnp.where(i == nb - 1, 2, 1)), 0, 0)

    def kv_spec(col, off):
        return pl.BlockSpec((A_BLOCK, LANES),
                            lambda b, i: (b * nb + jnp.clip(i + off, 0, nb - 1), col))

    return pl.pallas_call(
        _attn_a_kernel,
        grid=(bsz, nb),
        in_specs=[pl.BlockSpec(memory_space=pltpu.SMEM),
                  pl.BlockSpec((A_BLOCK, A_HEADS * HEAD_DIM), lambda b, i: (b * nb + i, 0)),
                  kv_spec(qcb, -1), kv_spec(qcb, 0), kv_spec(qcb, 1),
                  kv_spec(qcb + 1, -1), kv_spec(qcb + 1, 0), kv_spec(qcb + 1, 1),
                  pl.BlockSpec((CTX_LEN, LANES), lambda b, i: (ctx_blk0 + b, qcb)),
                  pl.BlockSpec((CTX_LEN, LANES), lambda b, i: (ctx_blk0 + b, qcb + 1)),
                  pl.BlockSpec((1, A_BLOCK, 3 * A_BLOCK), mask_idx)],
        out_specs=pl.BlockSpec((A_BLOCK, BRANCH_W), lambda b, i: (b * nb + i, 0)),
        out_shape=jax.ShapeDtypeStruct((out_rows, BRANCH_W), BF16),
        compiler_params=_cparams(("parallel", "parallel")),
        name="attn_window",
    )(sink, za, za, za, za, za, za, za, za, za, _window_masks())


NA_QROWS = 4
NA_KROWS = 12
NA_QTOK = NA_QROWS * GRID_W
NA_WIN = NA_KROWS * GRID_W
NA_KEYS = NA_WIN + CTX_LEN


def _attn_b_kernel(q_ref, k0_ref, k1_ref, k2_ref, v0_ref, v1_ref, v2_ref, kx_ref, vx_ref, bias_ref, o_ref):
    scale = HEAD_DIM ** -0.5

    def scores(h):
        hs = slice(h * HEAD_DIM, (h + 1) * HEAD_DIM)
        q = q_ref[:, hs] * scale
        k = jnp.concatenate([k0_ref[:, hs], k1_ref[:, hs], k2_ref[:, hs], kx_ref[:, hs]], axis=0)
        return lax.dot_general(q, k, _NT, preferred_element_type=F32)

    outs = []
    ones = jnp.ones((NA_KEYS, HEAD_DIM), BF16)
    s_next = scores(0)
    for h in range(B_HEADS):
        hs = slice(h * HEAD_DIM, (h + 1) * HEAD_DIM)
        s = jnp.concatenate([s_next[:, :NA_WIN] + bias_ref[0, h], s_next[:, NA_WIN:]], axis=1)
        v = jnp.concatenate([v0_ref[:, hs], v1_ref[:, hs], v2_ref[:, hs], vx_ref[:, hs]], axis=0)
        m = jnp.max(s, axis=-1, keepdims=True)
        p = jnp.exp(s - m).astype(BF16)
        if h + 1 < B_HEADS:
            s_next = scores(h + 1)
        ov = jnp.dot(p, jnp.concatenate([v, ones], axis=1), preferred_element_type=F32)
        outs.append(ov[:, :HEAD_DIM] / ov[:, HEAD_DIM:])
    o_ref[...] = jnp.concatenate(outs, axis=1).astype(o_ref.dtype)


def _bias_expand_kernel(rb_ref, onehot_ref, o_ref):
    o_ref[...] = jnp.dot(rb_ref[...], onehot_ref[...], preferred_element_type=F32,
                         precision=lax.Precision.HIGHEST)


def _na_bias_classes(rel_bias_all):
    rows = SEQ // GRID_W
    n_dr, n_dc = 2 * NA_KH - 1, 2 * NA_KW - 1
    qc = np.arange(GRID_W)[:, None]
    kc = np.arange(GRID_W)[None, :]
    cstart = np.clip(qc - NA_KW // 2, 0, GRID_W - NA_KW)
    col_ok = (kc >= cstart) & (kc < cstart + NA_KW)
    dc_idx = np.clip(kc - qc, -(NA_KW - 1), NA_KW - 1) + (NA_KW - 1)
    onehot = (np.arange(n_dc + 1)[:, None] == dc_idx.reshape(1, -1)).astype(np.float32)
    n_tab = DEPTH * B_HEADS * n_dr
    rb = jnp.pad(rel_bias_all.astype(F32).reshape(n_tab, n_dc), ((0, 0), (0, 1)))
    dense = pl.pallas_call(
        _bias_expand_kernel,
        grid=(DEPTH,),
        in_specs=[pl.BlockSpec((n_tab // DEPTH, n_dc + 1), lambda l: (l, 0)),
                  pl.BlockSpec((n_dc + 1, GRID_W * GRID_W), lambda l: (0, 0))],
        out_specs=pl.BlockSpec((n_tab // DEPTH, GRID_W * GRID_W), lambda l: (l, 0)),
        out_shape=jax.ShapeDtypeStruct((n_tab, GRID_W * GRID_W), F32),
        compiler_params=_cparams(("parallel",)),
        name="bias_expand",
    )(rb, jnp.asarray(onehot))
    dense = dense.reshape(DEPTH, B_HEADS, n_dr, GRID_W, GRID_W)
    masked = jnp.where(col_ok[None, None, None], dense, NEG_INF)
    n_groups = rows // NA_QROWS
    out = []
    for g in (0, 1, n_groups - 1):
        base = _na_key_base(g)
        per_row = []
        for lr in range(NA_QROWS):
            r = g * NA_QROWS + lr
            r0 = min(max(r - NA_KH // 2, 0), rows - NA_KH)
            dr_lo = r0 - r + (NA_KH - 1)
            before = r0 - base
            piece = jnp.pad(masked[:, :, dr_lo:dr_lo + NA_KH],
                            ((0, 0), (0, 0), (before, NA_KROWS - NA_KH - before), (0, 0), (0, 0)),
                            constant_values=NEG_INF)
            per_row.append(piece.transpose(0, 1, 3, 2, 4).reshape(DEPTH, B_HEADS, GRID_W, NA_KROWS * GRID_W))
        out.append(jnp.concatenate(per_row, axis=2))
    return jnp.stack(out, axis=1)


def _na_key_base(g):
    rows = SEQ // GRID_W
    return min(max(g * NA_QROWS - NA_KH // 2, 0), rows - NA_KROWS)


def _attn_b(zb, bias_cls, bsz, out_rows):
    n_groups = SEQ // NA_QTOK
    ctx_blk0 = bsz * SEQ // CTX_LEN
    max_base_blk = _na_key_base(n_groups - 1) // NA_QROWS

    def cls_idx(g, b):
        return (jnp.where(g == 0, 0, jnp.where(g == n_groups - 1, 2, 1)), 0, 0, 0)

    def kv_spec(col, j):
        return pl.BlockSpec((NA_QTOK, BRANCH_W),
                            lambda g, b: (b * n_groups + jnp.clip(g - 1, 0, max_base_blk) + j, col))

    return pl.pallas_call(
        _attn_b_kernel,
        grid=(n_groups, bsz),
        in_specs=[pl.BlockSpec((NA_QTOK, BRANCH_W), lambda g, b: (b * n_groups + g, 0)),
                  kv_spec(1, 0), kv_spec(1, 1), kv_spec(1, 2),
                  kv_spec(2, 0), kv_spec(2, 1), kv_spec(2, 2),
                  pl.BlockSpec((CTX_LEN, BRANCH_W), lambda g, b: (ctx_blk0 + b, 1)),
                  pl.BlockSpec((CTX_LEN, BRANCH_W), lambda g, b: (ctx_blk0 + b, 2)),
                  pl.BlockSpec((1, B_HEADS, NA_QTOK, NA_WIN), cls_idx)],
        out_specs=pl.BlockSpec((NA_QTOK, BRANCH_W), lambda g, b: (b * n_groups + g, 0)),
        out_shape=jax.ShapeDtypeStruct((out_rows, BRANCH_W), BF16),
        compiler_params=_cparams(("arbitrary", "arbitrary")),
        name="attn_neighbourhood",
    )(zb, zb, zb, zb, zb, zb, zb, zb, zb, bias_cls)


def _attn_ctx_kernel(sink_ref, q_ref, k_ref, v_ref, lat_ref, o_ref, *, n_heads, n_kv):
    del lat_ref
    grp = n_heads // n_kv
    rows = grp * CTX_LEN
    scale = HEAD_DIM ** -0.5
    head_of_row = lax.broadcasted_iota(jnp.int32, (rows, 1), 0) >> 8
    for g in range(n_kv):
        hs = slice(g * HEAD_DIM, (g + 1) * HEAD_DIM)
        q = jnp.concatenate([q_ref[:, (g * grp + j) * HEAD_DIM:(g * grp + j + 1) * HEAD_DIM]
                             for j in range(grp)], axis=0)
        s = lax.dot_general(q, k_ref[:, hs], _NT, preferred_element_type=F32) * scale
        sink = jnp.zeros((rows, 1), F32)
        for j in range(grp):
            sink = jnp.where(head_of_row == j, sink_ref[g * grp + j], sink)
        m = jnp.maximum(jnp.max(s, axis=-1, keepdims=True), sink)
        p = jnp.exp(s - m)
        den = jnp.sum(p, axis=-1, keepdims=True) + jnp.exp(sink - m)
        o = jnp.dot(p.astype(BF16), v_ref[:, hs], preferred_element_type=F32) / den
        o = jnp.concatenate([o[j * CTX_LEN:(j + 1) * CTX_LEN] for j in range(grp)], axis=1)
        o_ref[:, g * grp * HEAD_DIM:(g + 1) * grp * HEAD_DIM] = o.astype(o_ref.dtype)


def _attn_ctx(z, sink, y_lat, bsz, n_heads, n_kv):
    ctx_blk0 = bsz * SEQ // CTX_LEN
    kvw = n_kv * HEAD_DIM
    kcol = n_heads * HEAD_DIM // kvw
    return pl.pallas_call(
        functools.partial(_attn_ctx_kernel, n_heads=n_heads, n_kv=n_kv),
        grid=(bsz,),
        in_specs=[pl.BlockSpec(memory_space=pltpu.SMEM),
                  pl.BlockSpec((CTX_LEN, n_heads * HEAD_DIM), lambda b: (ctx_blk0 + b, 0)),
                  pl.BlockSpec((CTX_LEN, kvw), lambda b: (ctx_blk0 + b, kcol)),
                  pl.BlockSpec((CTX_LEN, kvw), lambda b: (ctx_blk0 + b, kcol + 1)),
                  pl.BlockSpec(memory_space=pl.ANY)],
        out_specs=pl.BlockSpec((CTX_LEN, BRANCH_W), lambda b: (ctx_blk0 + b, 0)),
        out_shape=jax.ShapeDtypeStruct(y_lat.shape, y_lat.dtype),
        input_output_aliases={4: 0},
        compiler_params=_cparams(("parallel",)),
        name="attn_ctx_%d" % n_kv,
    )(sink, z, z, z, y_lat)


def _cumulative_decay(g, tri):
    trib = tri.astype(BF16)
    g_hi = g.astype(BF16)
    rem = g - g_hi.astype(F32)
    g_mid = rem.astype(BF16)
    g_lo = (rem - g_mid.astype(F32)).astype(BF16)
    return (jnp.dot(trib, g_hi, preferred_element_type=F32) + jnp.dot(trib, g_mid, preferred_element_type=F32)
            + jnp.dot(trib, g_lo, preferred_element_type=F32))


def _scan_chunks(chains, cums, tris):
    def scaled_operands(chain, cum):
        q, k, v, state, d, n_heads, dk = chain
        t = q.shape[0]
        tot = cum[t - 1:t] if d == 0 else cum[0:1]
        mid = 0.5 * tot
        e_mid = jnp.exp(mid)
        qa = q * jnp.exp(cum - mid)
        kb = k * jnp.exp(mid - cum)
        qd = (qa * e_mid).astype(BF16)
        kd = (kb * e_mid).astype(BF16)
        return (qa.astype(BF16), kb.astype(BF16), qd, kd, v.astype(BF16),
                state * jnp.exp(tot), state.astype(BF16))

    def unmasked_matmuls(chain, ops):
        q, k, v, state, d, n_heads, dk = chain
        qa, kb, qd, kd, vb, decayed, state_b = ops
        per_head = []
        for h in range(n_heads):
            ks = slice(h * dk, (h + 1) * dk)
            vs = slice(h * LANES, (h + 1) * LANES)
            a = lax.dot_general(qa[:, ks], kb[:, ks], _NT, preferred_element_type=F32)
            carry = lax.dot_general(qd[:, ks], state_b[:, ks], _NT, preferred_element_type=F32)
            upd = lax.dot_general(vb[:, vs], kd[:, ks], _TN, preferred_element_type=F32)
            per_head.append((a, carry, decayed[:, ks] + upd))
        return per_head

    def masked_matmuls(chain, ops, per_head):
        d, vb = chain[4], ops[4]
        outs = []
        for h, (a, carry, new_s) in enumerate(per_head):
            a = jnp.where(tris[d], a, 0.0).astype(BF16)
            outs.append(jnp.dot(a, vb[:, h * LANES:(h + 1) * LANES], preferred_element_type=F32) + carry)
        return jnp.concatenate(outs, axis=1), jnp.concatenate([p[2] for p in per_head], axis=1)

    n = len(chains)
    ops, partial, results = [None] * n, [None] * n, [None] * n
    for i in range(n + 1):
        if i < n:
            ops[i] = scaled_operands(chains[i], cums[i])
            partial[i] = unmasked_matmuls(chains[i], ops[i])
        if i > 0:
            results[i - 1] = masked_matmuls(chains[i - 1], ops[i - 1], partial[i - 1])
    return results


def _scan_tokens(ref_chains):
    n_groups = SCAN_T // SCAN_GROUP
    rows = lax.broadcasted_iota(jnp.int32, (SCAN_GROUP, 1), 0)

    def group_body(gi, carry):
        for (q_ref, q_off, k_ref, k_off, v_ref, g_ref, o_ref, s_ref, d, n_heads, dk) in ref_chains:
            w = n_heads * dk
            g_idx = gi if d == 0 else n_groups - 1 - gi
            base = pl.multiple_of(g_idx * SCAN_GROUP, SCAN_GROUP)
            q16 = q_ref[pl.ds(base, SCAN_GROUP), q_off:q_off + w].astype(F32)
            k16 = k_ref[pl.ds(base, SCAN_GROUP), k_off:k_off + w].astype(F32)
            v16 = v_ref[pl.ds(base, SCAN_GROUP), :].astype(F32)
            g16 = g_ref[pl.ds(base, SCAN_GROUP), :]

            def token_body(jj, out16, q16=q16, k16=k16, v16=v16, g16=g16, s_ref=s_ref, d=d, n_heads=n_heads,
                           dk=dk):
                j = jj if d == 0 else SCAN_GROUP - 1 - jj
                sel = rows == j
                decay = jnp.exp(jnp.sum(jnp.where(sel, g16, 0.0), axis=0, keepdims=True))
                s = s_ref[...] * decay
                kj = jnp.where(sel, k16, 0.0)
                qj = jnp.where(sel, q16, 0.0)
                outs, cols = [], []
                for h in range(n_heads):
                    ks = slice(h * dk, (h + 1) * dk)
                    vs = slice(h * LANES, (h + 1) * LANES)
                    s_h = s[:, ks] + lax.dot_general(v16[:, vs], kj[:, ks], _TN, preferred_element_type=F32)
                    cols.append(s_h)
                    outs.append(lax.dot_general(qj[:, ks], s_h, _NT, preferred_element_type=F32))
                s_ref[...] = jnp.concatenate(cols, axis=1)
                return out16 + jnp.concatenate(outs, axis=1)

            out16 = lax.fori_loop(0, SCAN_GROUP, token_body, jnp.zeros((SCAN_GROUP, BRANCH_W), F32))
            o_ref[pl.ds(base, SCAN_GROUP), :] = out16
        return carry

    lax.fori_loop(0, n_groups, group_body, 0)


def _scan_kernel(cb_f, cg_f, cb_b, cg_b, db_f, dg_f, db_b, dg_b,
                 ocf_ref, ocb_ref, odf_ref, odb_ref, scf_scr, scb_scr, sdf_scr, sdb_scr):
    scratches = (scf_scr, scb_scr, sdf_scr, sdb_scr)

    @pl.when(pl.program_id(1) == 0)
    def _():
        for s in scratches:
            s[...] = jnp.zeros_like(s)

    dirs = tuple((cb.at[:, 0:CW], cb.at[:, (1 + d) * CW:(2 + d) * CW], cb.at[:, 3 * CW:4 * CW], cg,
                  db.at[:, 0:2 * DKW], db.at[:, 2 * DKW:2 * DKW + BRANCH_W], dg)
                 for d, (cb, cg, db, dg) in enumerate(((cb_f, cg_f, db_f, dg_f), (cb_b, cg_b, db_b, dg_b))))
    total = jnp.zeros((1, LANES), F32)
    for g_ref in (cg_f, cg_b, dg_f, dg_b):
        col_tot = jnp.sum(g_ref[...], axis=0, keepdims=True)
        for j in range(col_tot.shape[1] // LANES):
            total = jnp.minimum(total, col_tot[:, j * LANES:(j + 1) * LANES])
    in_range = jnp.min(total) > -SCAN_SAFE_RANGE

    t = SCAN_T
    row = lax.broadcasted_iota(jnp.int32, (t, t), 0)
    col = lax.broadcasted_iota(jnp.int32, (t, t), 1)
    tris = (col <= row, col >= row)
    cums = []
    for d, (cq, ck, ci, cg, dqk, dv, dg) in enumerate(dirs):
        cums += [_cumulative_decay(cg[...], tris[d]), _cumulative_decay(dg[...], tris[d])]

    @pl.when(in_range)
    def _():
        states = [s[...] for s in scratches]
        chains = []
        for d, (cq, ck, ci, cg, dqk, dv, dg) in enumerate(dirs):
            chains.append((cq[...].astype(F32), ck[...].astype(F32), ci[...], states[d], d, C_HEADS, C_DK))
            qk = dqk[...].astype(F32)
            chains.append((qk[:, :DKW], qk[:, DKW:], dv[...], states[2 + d], d, D_HEADS, D_DK))
        (ocf, scf), (odf, sdf), (ocb, scb), (odb, sdb) = _scan_chunks(chains, cums, tris)
        for ref, val in ((ocf_ref, ocf), (ocb_ref, ocb), (odf_ref, odf), (odb_ref, odb),
                         (scf_scr, scf), (scb_scr, scb), (sdf_scr, sdf), (sdb_scr, sdb)):
            ref[...] = val

    @pl.when(jnp.logical_not(in_range))
    def _():
        ref_chains = []
        for d, (cq, ck, ci, cg, dqk, dv, dg) in enumerate(dirs):
            ref_chains.append((cq, 0, ck, 0, ci, cg, (ocf_ref, ocb_ref)[d], scratches[d], d, C_HEADS, C_DK))
            ref_chains.append((dqk, 0, dqk, DKW, dv, dg, (odf_ref, odb_ref)[d], scratches[2 + d], d,
                               D_HEADS, D_DK))
        _scan_tokens(ref_chains)


def _scan(zcb, zcg, zdb, zdg, bsz):
    t = SCAN_T
    n_ctx = CTX_LEN // t
    n_lat = SEQ // t
    ctx_blk0 = bsz * n_lat

    def fwd(b, c):
        return jnp.where(c < n_ctx, ctx_blk0 + b * n_ctx + c, b * n_lat + c - n_ctx)

    def bwd(b, c):
        return jnp.where(c < n_ctx, ctx_blk0 + b * n_ctx + (n_ctx - 1 - c), b * n_lat + (n_lat + n_ctx - 1 - c))

    def spec(order, width, col):
        return pl.BlockSpec((t, width), lambda b, c: (order(b, c), col))

    in_specs = [spec(fwd, 4 * CW, 0), spec(fwd, CW, 0), spec(bwd, 4 * CW, 0), spec(bwd, CW, 1),
                spec(fwd, 2 * DKW + BRANCH_W, 0), spec(fwd, DKW, 0),
                spec(bwd, 2 * DKW + BRANCH_W, 0), spec(bwd, DKW, 1)]
    rows = zcb.shape[0]
    out = jax.ShapeDtypeStruct((rows, BRANCH_W), F32)
    return pl.pallas_call(
        _scan_kernel,
        grid=(bsz, n_ctx + n_lat),
        in_specs=in_specs,
        out_specs=[spec(fwd, 512, 0), spec(bwd, 512, 0), spec(fwd, 512, 0), spec(bwd, 512, 0)],
        out_shape=[out, out, out, out],
        scratch_shapes=[pltpu.VMEM((LANES, C_HEADS * C_DK), F32), pltpu.VMEM((LANES, C_HEADS * C_DK), F32),
                        pltpu.VMEM((LANES, D_HEADS * D_DK), F32), pltpu.VMEM((LANES, D_HEADS * D_DK), F32)],
        compiler_params=_cparams(("parallel", "arbitrary")),
        name="bidir_scan",
    )(zcb, zcg, zcb, zcg, zdb, zdg, zdb, zdg)


def _post_kernel(x_ref, mod_ref, gain_ref, ya_ref, yb_ref, ocf_ref, ocb_ref, odf_ref, odb_ref,
                 cg_ref, dg_ref, cn_ref, dn_ref, wm_ref, wb_ref, wo_ref, o_ref):
    m = mod_ref[0]

    def gated_group_norm(o, gain, gate):
        parts = [_rms(o[:, j * LANES:(j + 1) * LANES], gain) for j in range(BRANCH_W // LANES)]
        return (jnp.concatenate(parts, axis=1) * _silu(gate.astype(F32))).astype(BF16)

    for r in range(TM_POST // SUB_ROWS):
        rs = slice(r * SUB_ROWS, (r + 1) * SUB_ROWS)
        x = x_ref[rs, :]
        h = _norm_mod(x, gain_ref[0:1, :], m[1:2, :], m[0:1, :]).astype(BF16)
        y_c = gated_group_norm(ocf_ref[rs, :] + ocb_ref[rs, :], cn_ref[...], cg_ref[rs, :])
        y_d = gated_group_norm(odf_ref[rs, :] + odb_ref[rs, :], dn_ref[...], dg_ref[rs, :])
        branches = (ya_ref[rs, :], yb_ref[rs, :], y_c, y_d)
        acc = None
        for j, yb in enumerate(branches):
            gate = jnp.dot(h, wm_ref[:, j * D_MODEL:(j + 1) * D_MODEL], preferred_element_type=F32)
            term = jax.nn.sigmoid(gate) * jnp.dot(yb, wb_ref[j], preferred_element_type=F32)
            acc = term if acc is None else acc + term
        y = jnp.dot(acc.astype(BF16), wo_ref[...], preferred_element_type=F32)
        o_ref[rs, :] = x + m[2:3, :] * _rms(y, gain_ref[1:2, :])


def _post(xs, mod, gains, ya, yb, ocf, ocb, odf, odb, zc, zd, c_norm, d_norm, wm, wb, wo, bsz, n_rows):
    tm = TM_POST
    n_lat_tiles = bsz * SEQ // tm
    tiles_per_seq = SEQ // tm

    def mod_idx(i):
        return (jnp.where(i < n_lat_tiles, i // tiles_per_seq, bsz), 0, 0)

    row = lambda w, col=0: pl.BlockSpec((tm, w), lambda i: (i, col))
    const = _const_spec
    return pl.pallas_call(
        _post_kernel,
        grid=(n_rows // tm,),
        in_specs=[row(D_MODEL), pl.BlockSpec((1, 8, D_MODEL), mod_idx), const((4, D_MODEL)),
                  row(BRANCH_W), row(BRANCH_W), row(BRANCH_W), row(BRANCH_W), row(BRANCH_W), row(BRANCH_W),
                  row(BRANCH_W, 4), row(BRANCH_W, 2),
                  const((1, LANES)), const((1, LANES)),
                  const((D_MODEL, N_BRANCH * D_MODEL)), const((N_BRANCH, BRANCH_W, D_MODEL)),
                  const((D_MODEL, D_MODEL))],
        out_specs=row(D_MODEL),
        out_shape=jax.ShapeDtypeStruct((n_rows, D_MODEL), F32),
        compiler_params=_cparams(("parallel",)),
        name="merge_out",
    )(xs, mod, gains, ya, yb, ocf, ocb, odf, odb, zc, zd, c_norm, d_norm, wm, wb, wo)


def _ffn_kernel(x_ref, mod_ref, gain_ref, w1_ref, w2_ref, o_ref):
    m = mod_ref[0]
    for r in range(TM // SUB_ROWS):
        rs = slice(r * SUB_ROWS, (r + 1) * SUB_ROWS)
        x = x_ref[rs, :]
        h = _norm_mod(x, gain_ref[2:3, :], m[4:5, :], m[3:4, :]).astype(BF16)
        acc = None
        for k in range(D_FF // FF_CHUNK):
            ks = slice(k * FF_CHUNK, (k + 1) * FF_CHUNK)
            u = jnp.dot(h, w1_ref[:, ks], preferred_element_type=F32)
            u = jnp.square(jnp.maximum(u, 0.0)).astype(BF16)
            part = jnp.dot(u, w2_ref[ks, :], preferred_element_type=F32)
            acc = part if acc is None else acc + part
        o_ref[rs, :] = x + m[5:6, :] * _rms(acc, gain_ref[3:4, :])


def _ffn(xs, mod, gains, w1, w2, bsz, n_rows):
    n_lat_tiles = bsz * SEQ // TM
    tiles_per_seq = SEQ // TM

    def mod_idx(i):
        return (jnp.where(i < n_lat_tiles, i // tiles_per_seq, bsz), 0, 0)

    return pl.pallas_call(
        _ffn_kernel,
        grid=(n_rows // TM,),
        in_specs=[pl.BlockSpec((TM, D_MODEL), lambda i: (i, 0)),
                  pl.BlockSpec((1, 8, D_MODEL), mod_idx),
                  _const_spec((4, D_MODEL)),
                  _const_spec((D_MODEL, D_FF)),
                  _const_spec((D_FF, D_MODEL))],
        out_specs=pl.BlockSpec((TM, D_MODEL), lambda i: (i, 0)),
        out_shape=jax.ShapeDtypeStruct((n_rows, D_MODEL), F32),
        compiler_params=_cparams(("parallel",)),
        name="mlp",
    )(xs, mod, gains, w1, w2)


def kernel(x, c, ctx, c_ctx, w_mod, b_mod, norm_gains, w_in, a_sink, b_rel_bias, c_lower_bounds, c_norm,
           d_gate_up, d_gate_bias, d_norm, w_branch, w_out, w_ff1, w_ff2):
    bsz = x.shape[0]
    n_lat = bsz * SEQ
    n_rows = n_lat + bsz * CTX_LEN

    lb_soft = jax.nn.softmax(c_lower_bounds.astype(F32), axis=0)
    lb_all = jnp.cumsum(lb_soft, axis=0) - lb_soft[0:1]

    c_rows = jnp.concatenate([c, c_ctx[None, :], jnp.zeros((-(bsz + 1) % 8, D_MODEL), F32)], axis=0)
    mod_all = _modulation(c_rows, w_mod, b_mod)[:, :bsz + 1]
    mod_all = mod_all.reshape(DEPTH, bsz + 1, 6, D_MODEL)
    mod_all = jnp.pad(mod_all, ((0, 0), (0, 0), (0, 2), (0, 0)))

    rope = _rope_tables()
    bias_cls = _na_bias_classes(b_rel_bias)
    no_sink = jnp.full((B_HEADS,), NEG_INF, F32)
    xs = jnp.concatenate([x.reshape(n_lat, D_MODEL), ctx.reshape(bsz * CTX_LEN, D_MODEL)], axis=0)

    for l in range(DEPTH):
        need_ctx = l < DEPTH - 1
        mod, gains = mod_all[l], norm_gains[l]
        w_proj = w_in[l, :, :W_IN_D[1]].astype(BF16)
        w_m = w_in[l, :, W_IN_MERGE[0]:W_IN_MERGE[1]].astype(BF16)
        za, zb, zcb, zcg, zdb, zdg = _in_proj(xs, mod, gains, w_proj, rope,
                                              lb_all[l].reshape(1, -1), d_gate_up[l],
                                              d_gate_bias[l].reshape(2, 1, -1), bsz)

        rows_l = n_rows if need_ctx else n_lat
        ya = _attn_a(za, a_sink[l], bsz, rows_l)
        yb = _attn_b(zb, bias_cls[l], bsz, rows_l)
        if need_ctx:
            ya = _attn_ctx(za, a_sink[l], ya, bsz, A_HEADS, A_KV_HEADS)
            yb = _attn_ctx(zb, no_sink, yb, bsz, B_HEADS, B_HEADS)
        ocf, ocb, odf, odb = _scan(zcb, zcg, zdb, zdg, bsz)

        x_mid = _post(xs, mod, gains, ya, yb, ocf, ocb, odf, odb, zcb, zdb,
                      c_norm[l].reshape(1, -1), d_norm[l].reshape(1, -1),
                      w_m, w_branch[l].astype(BF16), w_out[l].astype(BF16), bsz, rows_l)
        xs = _ffn(x_mid, mod, gains, w_ff1[l].astype(BF16), w_ff2[l].astype(BF16), bsz, rows_l)
    return xs[:n_lat].reshape(bsz, SEQ, D_MODEL)
```
